```python
import math
import jax, jax.numpy as jnp
from jax import lax
import numpy as np

D_MODEL = 2048
BATCH = 4
SEQ = 4096
DEPTH = 2

GRID_W = 64
CTX_LEN = 256
W_CONV = D_MODEL // 4
W_HYENA = D_MODEL // 4
N_DIFF_HEADS = 8
DIFF_HEAD_DIM = 64
W_DIFF = N_DIFF_HEADS * 2 * DIFF_HEAD_DIM
N_BRANCH = 3
N_IN = 2 * W_CONV + 3 * W_HYENA + 3 * W_DIFF + N_BRANCH * D_MODEL
CONV_K = 31
SHORT_K = 3
HYENA_ORDER = 2
HYENA_BANDS = 8
HYENA_EMB = 1 + 2 * HYENA_BANDS
HYENA_FILT = 64
HYENA_TARGET = 1e-2
HYENA_FAST_PCT = 0.3
HYENA_SLOW_PCT = 1.5
D_FF = 5632
FFN_K = 3
ROPE_BASE = 10000.0
Q_BLOCK = 128
EPS = 1e-6

kernel_name = 'hybrid_conv_hyena_diffattn_dit'


def rms_norm(x, g):
    xf = x.astype(jnp.float32)
    y = xf * lax.rsqrt(jnp.mean(xf * xf, axis=-1, keepdims=True) + EPS)
    return (y * g.astype(jnp.float32)).astype(x.dtype)


def layer_norm(x, g, b):
    xf = x.astype(jnp.float32)
    mu = jnp.mean(xf, axis=-1, keepdims=True)
    var = jnp.mean(jnp.square(xf - mu), axis=-1, keepdims=True)
    y = (xf - mu) * lax.rsqrt(var + EPS) * g.astype(jnp.float32) + b.astype(jnp.float32)
    return y.astype(x.dtype)


def modulate(h, shift, scale):
    return h * (1 + scale) + shift


def depthwise_conv(x, w, b):
    k = w.shape[0]
    pad = (k - 1) // 2
    y = lax.conv_general_dilated(x, w[:, None, :].astype(x.dtype), (1,), [(pad, pad)],
                                 dimension_numbers=('NWC', 'WIO', 'NWC'),
                                 feature_group_count=x.shape[-1])
    return y + b


def axial_rope(rows):
    row = jnp.repeat(jnp.arange(rows, dtype=jnp.int32), GRID_W).astype(jnp.float32)
    col = jnp.tile(jnp.arange(GRID_W, dtype=jnp.int32), rows).astype(jnp.float32)
    n_freq = DIFF_HEAD_DIM // 4
    inv = ROPE_BASE ** (-jnp.arange(n_freq, dtype=jnp.float32) / n_freq)
    ang = jnp.concatenate([row[:, None] * inv, col[:, None] * inv], axis=-1)
    return jnp.cos(ang), jnp.sin(ang)


def apply_rope(x, cos, sin):
    half = DIFF_HEAD_DIM // 2
    x1 = x[..., :half].astype(jnp.float32)
    x2 = x[..., half:].astype(jnp.float32)
    c = cos[:, None, None, :]
    s = sin[:, None, None, :]
    return jnp.concatenate([x1 * c - x2 * s, x1 * s + x2 * c], axis=-1).astype(x.dtype)


def split_qk_heads(t):
    return t.reshape(t.shape[0], t.shape[1], N_DIFF_HEADS, 2, DIFF_HEAD_DIM)


def split_v_heads(t):
    return t.reshape(t.shape[0], t.shape[1], N_DIFF_HEADS, 2 * DIFF_HEAD_DIM)


def diff_softmax_mix(q, k, v, lam):
    s = jnp.einsum('bqhmd,bkhmd->bmhqk', q, k, preferred_element_type=jnp.float32)
    p = jax.nn.softmax(s * (DIFF_HEAD_DIM ** -0.5), axis=-1)
    a = p[:, 0] - lam * p[:, 1]
    return jnp.einsum('bhqk,bkhd->bqhd', a.astype(v.dtype), v)


def diff_output(o, p, lam_init):
    o = rms_norm(o, p['subln_g']) * (1.0 - lam_init)
    return o.reshape(o.shape[0], o.shape[1], W_DIFF) @ p['w_c_out']


def conformer_conv(u, p):
    a, gate = jnp.split(u, 2, axis=-1)
    y = a * jax.nn.sigmoid(gate)
    y = depthwise_conv(y, p['conv_a_w'], p['conv_a_b'])
    y = jax.nn.silu(layer_norm(y, p['ln_a_g'], p['ln_a_b']))
    return y @ p['w_a_out']


def hyena_filters(length, p):
    pos = jnp.arange(length, dtype=jnp.float32)
    t = pos / max(length - 1, 1)
    bands = jnp.linspace(1e-4, HYENA_BANDS - 1, HYENA_BANDS, dtype=jnp.float32)
    ang = (2.0 * math.pi * pos / length)[:, None] * bands[None, :]
    feats = jnp.concatenate([t[:, None], jnp.cos(ang), -jnp.sin(ang)], axis=-1)
    f32 = jnp.float32
    h = jnp.sin(feats @ p['filt_w1'].astype(f32) + p['filt_b1'].astype(f32))
    h = jnp.sin(h @ p['filt_w2'].astype(f32) + p['filt_b2'].astype(f32))
    h = h @ p['filt_w3'].astype(f32)
    deltas = jnp.abs(jnp.linspace(math.log(HYENA_TARGET) / HYENA_SLOW_PCT,
                                  math.log(HYENA_TARGET) / HYENA_FAST_PCT, W_HYENA, dtype=f32))
    decay = jnp.exp(-t[:, None] * deltas[None, :])
    h = h.reshape(length, HYENA_ORDER, 2, W_HYENA) * decay[:, None, None, :]
    h_fwd, h_bwd = h[:, :, 0], h[:, :, 1]
    buf = jnp.concatenate([h_fwd, jnp.zeros((1, HYENA_ORDER, W_HYENA), f32), h_bwd[1:][::-1]], axis=0)
    return buf / jnp.sum(jnp.abs(buf), axis=0, keepdims=True)


def hyena_branch(u, p):
    length = u.shape[1]
    n = 2 * length
    u = depthwise_conv(u, p['short_b_w'], p['short_b_b'])
    v, x1, x2 = jnp.split(u, 3, axis=-1)
    hf = jnp.fft.rfft(hyena_filters(length, p), n=n, axis=0)
    z = v
    for o, gate in enumerate((x1, x2)):
        zf = jnp.fft.rfft(z.astype(jnp.float32), n=n, axis=1)
        conv = jnp.fft.irfft(zf * hf[None, :, o, :], n=n, axis=1)[:, :length]
        z = gate * (conv.astype(z.dtype) + z * p['hyena_skip'][o])
    return z @ p['w_b_out']


def gated_merge(br_a, br_b, br_c, g_logits, p):
    g = jax.nn.sigmoid(g_logits + p['b_gate'])
    g_a, g_b, g_c = jnp.split(g, N_BRANCH, axis=-1)
    return (g_a * br_a + g_b * br_b + g_c * br_c) @ p['w_o']


def token_mixers(hx, hc, p, rope_cos, rope_sin, lam_init, update_ctx):
    cuts = [2 * W_CONV,
            2 * W_CONV + 3 * W_HYENA,
            2 * W_CONV + 3 * W_HYENA + W_DIFF,
            2 * W_CONV + 3 * W_HYENA + 2 * W_DIFF,
            2 * W_CONV + 3 * W_HYENA + 3 * W_DIFF]
    ax, bx, qx, kx, vx, gx = jnp.split(hx @ p['w_in'], cuts, axis=-1)
    ac, bc, qc, kc, vc, gc = jnp.split(hc @ p['w_in'], cuts, axis=-1)
    lq1, lk1, lq2, lk2 = p['diff_lambda'].astype(jnp.float32)
    lam = jnp.exp(jnp.sum(lq1 * lk1)) - jnp.exp(jnp.sum(lq2 * lk2)) + lam_init
    qx = apply_rope(split_qk_heads(qx), rope_cos, rope_sin)
    kx = apply_rope(split_qk_heads(kx), rope_cos, rope_sin)
    kc = split_qk_heads(kc)
    vx = split_v_heads(vx)
    vc = split_v_heads(vc)
    k_all = jnp.concatenate([kc, kx], axis=1)
    v_all = jnp.concatenate([vc, vx], axis=1)
    b, t = hx.shape[0], hx.shape[1]
    n_blk = t // Q_BLOCK
    q_blocks = jnp.moveaxis(qx.reshape(b, n_blk, Q_BLOCK, N_DIFF_HEADS, 2, DIFF_HEAD_DIM), 1, 0)
    ox = lax.map(lambda qb: diff_softmax_mix(qb, k_all, v_all, lam), q_blocks)
    ox = jnp.moveaxis(ox, 0, 1).reshape(b, t, N_DIFF_HEADS, 2 * DIFF_HEAD_DIM)
    out_x = gated_merge(conformer_conv(ax, p), hyena_branch(bx, p), diff_output(ox, p, lam_init), gx, p)
    if not update_ctx:
        return out_x, None
    oc = diff_softmax_mix(split_qk_heads(qc), kc, vc, lam)
    out_c = gated_merge(conformer_conv(ac, p), hyena_branch(bc, p), diff_output(oc, p, lam_init), gc, p)
    return out_x, out_c


def conv_ffn(h, p):
    u = depthwise_conv(h @ p['w_up'], p['conv_f_w'], p['conv_f_b'])
    a, v = jnp.split(u, 2, axis=-1)
    return (jax.nn.silu(a) * v) @ p['w_down']


def setup_inputs(seed: int = 0) -> dict:
    key = jax.random.key(seed)
    ks = jax.random.split(key, 40)
    counter = [0]

    def nrm(shape, scale):
        k = ks[counter[0]]
        counter[0] += 1
        return jax.random.normal(k, shape, jnp.float32) * scale

    L, D = DEPTH, D_MODEL
    return {
        'x': nrm((BATCH, SEQ, D), 1.0),
        'c': nrm((BATCH, D), 1.0),
        'ctx': nrm((BATCH, CTX_LEN, D), 1.0),
        'c_ctx': nrm((D,), 1.0),
        'w_ada': nrm((L, D, 6 * D), 0.5 * D ** -0.5),
        'b_ada': nrm((L, 6 * D), 0.02),
        'norm1_g': 1.0 + nrm((L, D), 0.02),
        'norm2_g': 1.0 + nrm((L, D), 0.02),
        'w_in': nrm((L, D, N_IN), D ** -0.5),
        'b_gate': nrm((L, N_BRANCH * D), 0.02),
        'conv_a_w': nrm((L, CONV_K, W_CONV), CONV_K ** -0.5),
        'conv_a_b': nrm((L, W_CONV), 0.02),
        'ln_a_g': 1.0 + nrm((L, W_CONV), 0.02),
        'ln_a_b': nrm((L, W_CONV), 0.02),
        'w_a_out': nrm((L, W_CONV, D), W_CONV ** -0.5),
        'short_b_w': nrm((L, SHORT_K, 3 * W_HYENA), SHORT_K ** -0.5),
        'short_b_b': nrm((L, 3 * W_HYENA), 0.02),
        'filt_w1': nrm((L, HYENA_EMB, HYENA_FILT), HYENA_EMB ** -0.5),
        'filt_b1': nrm((L, HYENA_FILT), 0.1),
        'filt_w2': nrm((L, HYENA_FILT, HYENA_FILT), HYENA_FILT ** -0.5),
        'filt_b2': nrm((L, HYENA_FILT), 0.1),
        'filt_w3': nrm((L, HYENA_FILT, HYENA_ORDER * 2 * W_HYENA), HYENA_FILT ** -0.5),
        'hyena_skip': nrm((L, HYENA_ORDER, W_HYENA), 1.0),
        'w_b_out': nrm((L, W_HYENA, D), W_HYENA ** -0.5),
        'diff_lambda': nrm((L, 4, DIFF_HEAD_DIM), 0.1),
        'subln_g': 1.0 + nrm((L, N_DIFF_HEADS, 2 * DIFF_HEAD_DIM), 0.02),
        'w_c_out': nrm((L, W_DIFF, D), W_DIFF ** -0.5),
        'w_o': nrm((L, D, D), D ** -0.5),
        'w_up': nrm((L, D, 2 * D_FF), D ** -0.5),
        'conv_f_w': nrm((L, FFN_K, 2 * D_FF), FFN_K ** -0.5),
        'conv_f_b': nrm((L, 2 * D_FF), 0.02),
        'w_down': nrm((L, D_FF, D), D_FF ** -0.5),
        'final_g': 1.0 + nrm((D,), 0.02),
    }


def reference(x, c, ctx, c_ctx, w_ada, b_ada, norm1_g, norm2_g, w_in, b_gate,
              conv_a_w, conv_a_b, ln_a_g, ln_a_b, w_a_out,
              short_b_w, short_b_b, filt_w1, filt_b1, filt_w2, filt_b2, filt_w3,
              hyena_skip, w_b_out, diff_lambda, subln_g, w_c_out, w_o,
              w_up, conv_f_w, conv_f_b, w_down, final_g):
    rows = x.shape[1] // GRID_W
    rope_cos, rope_sin = axial_rope(rows)
    for l in range(DEPTH):
        p = {
            'w_in': w_in[l], 'b_gate': b_gate[l],
            'conv_a_w': conv_a_w[l], 'conv_a_b': conv_a_b[l],
            'ln_a_g': ln_a_g[l], 'ln_a_b': ln_a_b[l], 'w_a_out': w_a_out[l],
            'short_b_w': short_b_w[l], 'short_b_b': short_b_b[l],
            'filt_w1': filt_w1[l], 'filt_b1': filt_b1[l], 'filt_w2': filt_w2[l],
            'filt_b2': filt_b2[l], 'filt_w3': filt_w3[l],
            'hyena_skip': hyena_skip[l], 'w_b_out': w_b_out[l],
            'diff_lambda': diff_lambda[l], 'subln_g': subln_g[l], 'w_c_out': w_c_out[l],
            'w_o': w_o[l], 'w_up': w_up[l], 'conv_f_w': conv_f_w[l],
            'conv_f_b': conv_f_b[l], 'w_down': w_down[l],
        }
        update_ctx = l < DEPTH - 1
        lam_init = 0.8 - 0.6 * math.exp(-0.3 * l)
        mx = jnp.split((jax.nn.silu(c) @ w_ada[l] + b_ada[l])[:, None, :], 6, axis=-1)
        mc = jnp.split((jax.nn.silu(c_ctx) @ w_ada[l] + b_ada[l])[None, None, :], 6, axis=-1)
        hx = modulate(rms_norm(x, norm1_g[l]), mx[0], mx[1])
        hc = modulate(rms_norm(ctx, norm1_g[l]), mc[0], mc[1])
        out_x, out_c = token_mixers(hx, hc, p, rope_cos, rope_sin, lam_init, update_ctx)
        x = x + mx[2] * out_x
        hx = modulate(rms_norm(x, norm2_g[l]), mx[3], mx[4])
        x = x + mx[5] * conv_ffn(hx, p)
        if update_ctx:
            ctx = ctx + mc[2] * out_c
            hc = modulate(rms_norm(ctx, norm2_g[l]), mc[3], mc[4])
            ctx = ctx + mc[5] * conv_ffn(hc, p)
    return rms_norm(x, final_g)
```

```python
import functools
import math

import jax
import jax.numpy as jnp
from jax import lax
from jax.experimental import pallas as pl
from jax.experimental.pallas import tpu as pltpu

D_MODEL = 2048
DEPTH = 2
GRID_W = 64
W_CONV = D_MODEL // 4
W_HYENA = D_MODEL // 4
N_HEADS = 8
HEAD_DIM = 64
W_DIFF = N_HEADS * 2 * HEAD_DIM
N_BRANCH = 3
HYENA_ORDER = 2
HYENA_BANDS = 8
HYENA_TARGET = 1e-2
HYENA_FAST_PCT = 0.3
HYENA_SLOW_PCT = 1.5
ROPE_BASE = 10000.0
EPS = 1e-6
LOG2E = 1.4426950408889634

VMEM_LIMIT_BYTES = 56 * 1024 * 1024


def _params(n_axes):
    return pltpu.CompilerParams(
        dimension_semantics=("arbitrary",) * n_axes, vmem_limit_bytes=VMEM_LIMIT_BYTES)


def _mm_kernel(x_ref, w_ref, o_ref):
    o_ref[...] = jnp.dot(x_ref[...], w_ref[...],
                         preferred_element_type=jnp.float32).astype(o_ref.dtype)


def _pick(n, prefs):
    for p in prefs:
        if n % p == 0:
            return p
    return n


def matmul(x, w, out_dtype=jnp.float32, name="mm"):
    m, k = x.shape
    _, n = w.shape
    tm = _pick(m, (1024, 512, 256, 128, 64, 32, 16, 8))
    tn = _pick(n, (512, 256, 128))
    return pl.pallas_call(
        _mm_kernel,
        out_shape=jax.ShapeDtypeStruct((m, n), out_dtype),
        grid=(m // tm, n // tn),
        in_specs=[pl.BlockSpec((tm, k), lambda i, j: (i, 0)),
                  pl.BlockSpec((k, tn), lambda i, j: (0, j))],
        out_specs=pl.BlockSpec((tm, tn), lambda i, j: (i, j)),
        compiler_params=_params(2),
        name=name,
    )(x.astype(jnp.bfloat16), w.astype(jnp.bfloat16))


def _attn_kernel(q_ref, k_ref, v_ref, dl_ref, g_ref, o_ref, *, tk, n_chunks, lam_init):
    q = q_ref[0, 0]
    tq = q.shape[0]
    q0 = q[:, :HEAD_DIM]
    q1 = q[:, HEAD_DIM:]

    def body(c, carry):
        m0, l0, a0, m1, l1, a1 = carry
        start = pl.multiple_of(c * tk, tk)
        kc = k_ref[0, 0, pl.ds(start, tk), :]
        vc = v_ref[0, 0, pl.ds(start, tk), :]

        def one(qm, km, m_prev, l_prev, a_prev):
            s = lax.dot_general(qm, km, (((1,), (1,)), ((), ())),
                                preferred_element_type=jnp.float32)
            m_new = jnp.maximum(m_prev, jnp.max(s, axis=-1, keepdims=True))
            alpha = jnp.exp2(m_prev - m_new)
            p = jnp.exp2(s - m_new)
            l_new = alpha * l_prev + jnp.sum(p, axis=-1, keepdims=True)
            a_new = alpha * a_prev + jnp.dot(p.astype(jnp.bfloat16), vc,
                                             preferred_element_type=jnp.float32)
            return m_new, l_new, a_new

        m0, l0, a0 = one(q0, kc[:, :HEAD_DIM], m0, l0, a0)
        m1, l1, a1 = one(q1, kc[:, HEAD_DIM:], m1, l1, a1)
        return m0, l0, a0, m1, l1, a1

    neg = jnp.full((tq, 1), -1e30, jnp.float32)
    zl = jnp.zeros((tq, 1), jnp.float32)
    za = jnp.zeros((tq, 2 * HEAD_DIM), jnp.float32)
    m0, l0, a0, m1, l1, a1 = lax.fori_loop(0, n_chunks, body, (neg, zl, za, neg, zl, za))

    dl = dl_ref[...]
    s1 = jnp.sum(dl[0:1] * dl[1:2], axis=-1, keepdims=True)
    s2 = jnp.sum(dl[2:3] * dl[3:4], axis=-1, keepdims=True)
    lam = jnp.exp(s1) - jnp.exp(s2) + lam_init
    o = a0 / l0 - lam * (a1 / l1)
    o = o * lax.rsqrt(jnp.mean(o * o, axis=-1, keepdims=True) + EPS)
    o_ref[0] = (o * g_ref[0] * (1.0 - lam_init)).astype(o_ref.dtype)


def diff_attention(q, k, v, diff_lambda, subln_g, lam_init):
    b, h, t_q, _ = q.shape
    t_k = k.shape[2]
    tq = 256
    tk = 256
    kern = functools.partial(_attn_kernel, tk=tk, n_chunks=t_k // tk, lam_init=lam_init)
    return pl.pallas_call(
        kern,
        out_shape=jax.ShapeDtypeStruct((b, t_q, h * 2 * HEAD_DIM), jnp.float32),
        grid=(b, h, t_q // tq),
        in_specs=[pl.BlockSpec((1, 1, tq, 2 * HEAD_DIM), lambda bi, hi, i: (bi, hi, i, 0)),
                  pl.BlockSpec((1, 1, t_k, 2 * HEAD_DIM), lambda bi, hi, i: (bi, hi, 0, 0)),
                  pl.BlockSpec((1, 1, t_k, 2 * HEAD_DIM), lambda bi, hi, i: (bi, hi, 0, 0)),
                  pl.BlockSpec((4, HEAD_DIM), lambda bi, hi, i: (0, 0)),
                  pl.BlockSpec((1, 1, 2 * HEAD_DIM), lambda bi, hi, i: (hi, 0, 0))],
        out_specs=pl.BlockSpec((1, tq, 2 * HEAD_DIM), lambda bi, hi, i: (bi, i, hi)),
        compiler_params=_params(3),
        name="diff_attn",
    )(q, k, v, diff_lambda.astype(jnp.float32),
      subln_g.astype(jnp.float32).reshape(h, 1, 2 * HEAD_DIM))


def rms_norm(x, g):
    y = x * lax.rsqrt(jnp.mean(x * x, axis=-1, keepdims=True) + EPS)
    return y * g


def layer_norm(x, g, b):
    mu = jnp.mean(x, axis=-1, keepdims=True)
    var = jnp.mean(jnp.square(x - mu), axis=-1, keepdims=True)
    return (x - mu) * lax.rsqrt(var + EPS) * g + b


def depthwise_conv(x, w, b):
    k = w.shape[0]
    pad = (k - 1) // 2
    y = lax.conv_general_dilated(x, w[:, None, :], (1,), [(pad, pad)],
                                 dimension_numbers=('NWC', 'WIO', 'NWC'),
                                 feature_group_count=x.shape[-1])
    return y + b


def axial_rope(rows):
    row = jnp.repeat(jnp.arange(rows, dtype=jnp.int32), GRID_W).astype(jnp.float32)
    col = jnp.tile(jnp.arange(GRID_W, dtype=jnp.int32), rows).astype(jnp.float32)
    n_freq = HEAD_DIM // 4
    inv = ROPE_BASE ** (-jnp.arange(n_freq, dtype=jnp.float32) / n_freq)
    ang = jnp.concatenate([row[:, None] * inv, col[:, None] * inv], axis=-1)
    return jnp.cos(ang), jnp.sin(ang)


def apply_rope(x, cos, sin):
    half = HEAD_DIM // 2
    x1 = x[..., :half]
    x2 = x[..., half:]
    c = cos[:, None, None, :]
    s = sin[:, None, None, :]
    return jnp.concatenate([x1 * c - x2 * s, x1 * s + x2 * c], axis=-1)


def mm3(x, w, name):
    b, t, k = x.shape
    return matmul(x.reshape(b * t, k), w, name=name).reshape(b, t, w.shape[1])


def conformer_conv(u, p):
    a, gate = jnp.split(u, 2, axis=-1)
    y = a * jax.nn.sigmoid(gate)
    y = depthwise_conv(y, p['conv_a_w'], p['conv_a_b'])
    y = jax.nn.silu(layer_norm(y, p['ln_a_g'], p['ln_a_b']))
    return mm3(y, p['w_a_out'], "mm_a_out")


def hyena_filters(length, p):
    hi = lax.Precision.HIGHEST
    pos = jnp.arange(length, dtype=jnp.float32)
    t = pos / max(length - 1, 1)
    bands = jnp.linspace(1e-4, HYENA_BANDS - 1, HYENA_BANDS, dtype=jnp.float32)
    ang = (2.0 * math.pi * pos / length)[:, None] * bands[None, :]
    feats = jnp.concatenate([t[:, None], jnp.cos(ang), -jnp.sin(ang)], axis=-1)
    h = jnp.sin(jnp.dot(feats, p['filt_w1'], precision=hi) + p['filt_b1'])
    h = jnp.sin(jnp.dot(h, p['filt_w2'], precision=hi) + p['filt_b2'])
    h = jnp.dot(h, p['filt_w3'], precision=hi)
    deltas = jnp.abs(jnp.linspace(math.log(HYENA_TARGET) / HYENA_SLOW_PCT,
                                  math.log(HYENA_TARGET) / HYENA_FAST_PCT, W_HYENA,
                                  dtype=jnp.float32))
    decay = jnp.exp(-t[:, None] * deltas[None, :])
    h = h.reshape(length, HYENA_ORDER, 2, W_HYENA) * decay[:, None, None, :]
    h_fwd, h_bwd = h[:, :, 0], h[:, :, 1]
    buf = jnp.concatenate([h_fwd, jnp.zeros((1, HYENA_ORDER, W_HYENA), jnp.float32),
                           h_bwd[1:][::-1]], axis=0)
    return buf / jnp.sum(jnp.abs(buf), axis=0, keepdims=True)


def hyena_branch(u, p):
    length = u.shape[1]
    n = 2 * length
    u = depthwise_conv(u, p['short_b_w'], p['short_b_b'])
    v, x1, x2 = jnp.split(u, 3, axis=-1)
    hf = jnp.fft.rfft(hyena_filters(length, p), n=n, axis=0)
    z = v
    for o, gate in enumerate((x1, x2)):
        zf = jnp.fft.rfft(z, n=n, axis=1)
        conv = jnp.fft.irfft(zf * hf[None, :, o, :], n=n, axis=1)[:, :length]
        z = gate * (conv + z * p['hyena_skip'][o])
    return mm3(z, p['w_b_out'], "mm_b_out")


def heads(t):
    b, tt, _ = t.shape
    return t.reshape(b, tt, N_HEADS, 2 * HEAD_DIM).transpose(0, 2, 1, 3)


def rope_heads(t, cos, sin):
    b, tt, _ = t.shape
    t = apply_rope(t.reshape(b, tt, N_HEADS, 2, HEAD_DIM), cos, sin)
    return t.reshape(b, tt, N_HEADS * 2 * HEAD_DIM)


def gated_merge(br_a, br_b, br_c, g_logits, p):
    g = jax.nn.sigmoid(g_logits + p['b_gate'])
    g_a, g_b, g_c = jnp.split(g, N_BRANCH, axis=-1)
    return mm3(g_a * br_a + g_b * br_b + g_c * br_c, p['w_o'], "mm_o")


def token_mixers(hx, hc, p, rope_cos, rope_sin, lam_init, update_ctx):
    cuts = [2 * W_CONV,
            2 * W_CONV + 3 * W_HYENA,
            2 * W_CONV + 3 * W_HYENA + W_DIFF,
            2 * W_CONV + 3 * W_HYENA + 2 * W_DIFF,
            2 * W_CONV + 3 * W_HYENA + 3 * W_DIFF]
    ax, bx, qx, kx, vx, gx = jnp.split(mm3(hx, p['w_in'], "mm_in_x"), cuts, axis=-1)
    ac, bc, qc, kc, vc, gc = jnp.split(mm3(hc, p['w_in'], "mm_in_c"), cuts, axis=-1)
    qscale = HEAD_DIM ** -0.5 * LOG2E
    bf = jnp.bfloat16
    qx = heads(rope_heads(qx, rope_cos, rope_sin) * qscale).astype(bf)
    k_all = heads(jnp.concatenate([kc, rope_heads(kx, rope_cos, rope_sin)], axis=1)).astype(bf)
    v_all = heads(jnp.concatenate([vc, vx], axis=1)).astype(bf)
    ox = diff_attention(qx, k_all, v_all, p['diff_lambda'], p['subln_g'], lam_init)
    br_c = mm3(ox, p['w_c_out'], "mm_c_out_x")
    out_x = gated_merge(conformer_conv(ax, p), hyena_branch(bx, p), br_c, gx, p)
    if not update_ctx:
        return out_x, None
    t_c = kc.shape[1]
    oc = diff_attention(heads(qc * qscale).astype(bf), k_all[:, :, :t_c], v_all[:, :, :t_c],
                        p['diff_lambda'], p['subln_g'], lam_init)
    br_cc = mm3(oc, p['w_c_out'], "mm_c_out_c")
    out_c = gated_merge(conformer_conv(ac, p), hyena_branch(bc, p), br_cc, gc, p)
    return out_x, out_c


def conv_ffn(h, p):
    u = depthwise_conv(mm3(h, p['w_up'], "mm_up"), p['conv_f_w'], p['conv_f_b'])
    a, v = jnp.split(u, 2, axis=-1)
    return mm3(jax.nn.silu(a) * v, p['w_down'], "mm_down")


def kernel(x, c, ctx, c_ctx, w_ada, b_ada, norm1_g, norm2_g, w_in, b_gate, conv_a_w, conv_a_b, ln_a_g, ln_a_b, w_a_out, short_b_w, short_b_b, filt_w1, filt_b1, filt_w2, filt_b2, filt_w3, hyena_skip, w_b_out, diff_lambda, subln_g, w_c_out, w_o, w_up, conv_f_w, conv_f_b, w_down, final_g):
    rows = x.shape[1] // GRID_W
    rope_cos, rope_sin = axial_rope(rows)
    hi = lax.Precision.HIGHEST
    for l in range(DEPTH):
        p = {
            'w_in': w_in[l], 'b_gate': b_gate[l],
            'conv_a_w': conv_a_w[l], 'conv_a_b': conv_a_b[l],
            'ln_a_g': ln_a_g[l], 'ln_a_b': ln_a_b[l], 'w_a_out': w_a_out[l],
            'short_b_w': short_b_w[l], 'short_b_b': short_b_b[l],
            'filt_w1': filt_w1[l], 'filt_b1': filt_b1[l], 'filt_w2': filt_w2[l],
            'filt_b2': filt_b2[l], 'filt_w3': filt_w3[l],
            'hyena_skip': hyena_skip[l], 'w_b_out': w_b_out[l],
            'diff_lambda': diff_lambda[l], 'subln_g': subln_g[l], 'w_c_out': w_c_out[l],
            'w_o': w_o[l], 'w_up': w_up[l], 'conv_f_w': conv_f_w[l],
            'conv_f_b': conv_f_b[l], 'w_down': w_down[l],
        }
        update_ctx = l < DEPTH - 1
        lam_init = 0.8 - 0.6 * math.exp(-0.3 * l)
        cc = jnp.concatenate([c, c_ctx[None, :]], axis=0)
        mod = jnp.dot(jax.nn.silu(cc), w_ada[l], precision=hi) + b_ada[l]
        mx = jnp.split(mod[:-1, None, :], 6, axis=-1)
        mc = jnp.split(mod[-1:, None, :], 6, axis=-1)
        hx = rms_norm(x, norm1_g[l]) * (1 + mx[1]) + mx[0]
        hc = rms_norm(ctx, norm1_g[l]) * (1 + mc[1]) + mc[0]
        out_x, out_c = token_mixers(hx, hc, p, rope_cos, rope_sin, lam_init, update_ctx)
        x = x + mx[2] * out_x
        hx = rms_norm(x, norm2_g[l]) * (1 + mx[4]) + mx[3]
        x = x + mx[5] * conv_ffn(hx, p)
        if update_ctx:
            ctx = ctx + mc[2] * out_c
            hc = rms_norm(ctx, norm2_g[l]) * (1 + mc[4]) + mc[3]
            ctx = ctx + mc[5] * conv_ffn(hc, p)
    return rms_norm(x, final_g)
```

```python
import functools
import math

import numpy as np
import jax
import jax.numpy as jnp
from jax import lax
from jax.experimental import pallas as pl
from jax.experimental.pallas import tpu as pltpu

D_MODEL = 2048
DEPTH = 2
GRID_W = 64
W_CONV = D_MODEL // 4
W_HYENA = D_MODEL // 4
N_HEADS = 8
HEAD_DIM = 64
W_DIFF = N_HEADS * 2 * HEAD_DIM
N_BRANCH = 3
HYENA_ORDER = 2
HYENA_BANDS = 8
HYENA_TARGET = 1e-2
HYENA_FAST_PCT = 0.3
HYENA_SLOW_PCT = 1.5
ROPE_BASE = 10000.0
EPS = 1e-6
LOG2E = 1.4426950408889634

LANE = 128
FREQ_BLOCK = 256
VMEM_LIMIT_BYTES = 56 * 1024 * 1024


def _params(n_axes):
    return pltpu.CompilerParams(
        dimension_semantics=("arbitrary",) * n_axes, vmem_limit_bytes=VMEM_LIMIT_BYTES)


def _mm_kernel(x_ref, w_ref, o_ref):
    o_ref[...] = jnp.dot(x_ref[...], w_ref[...],
                         preferred_element_type=jnp.float32).astype(o_ref.dtype)


def _pick(n, prefs):
    for p in prefs:
        if n % p == 0:
            return p
    return n


def matmul(x, w, out_dtype=jnp.float32, name="mm"):
    m, k = x.shape
    _, n = w.shape
    tm = _pick(m, (1024, 512, 256, 128, 64, 32, 16, 8))
    tn = _pick(n, (512, 256, 128))
    return pl.pallas_call(
        _mm_kernel,
        out_shape=jax.ShapeDtypeStruct((m, n), out_dtype),
        grid=(m // tm, n // tn),
        in_specs=[pl.BlockSpec((tm, k), lambda i, j: (i, 0)),
                  pl.BlockSpec((k, tn), lambda i, j: (0, j))],
        out_specs=pl.BlockSpec((tm, tn), lambda i, j: (i, j)),
        compiler_params=_params(2),
        name=name,
    )(x.astype(jnp.bfloat16), w.astype(jnp.bfloat16))


def _attn_kernel(qt_ref, k_ref, vt_ref, dl_ref, g_ref, o_ref, s_ref, p_ref, *, lam_init, n_sub):
    kk = k_ref[0, 0]
    vt = vt_ref[0, 0]
    tq = qt_ref.shape[-1]
    ts = tq // n_sub
    streams = [(mi, si) for si in range(n_sub) for mi in range(2)]
    ms = {}
    for mi, si in streams:
        s = jnp.dot(kk, qt_ref[0, 0, mi, :, si * ts:(si + 1) * ts],
                    preferred_element_type=jnp.float32)
        s_ref[mi, si] = s
        ms[mi, si] = jnp.max(s, axis=0, keepdims=True)
    outs = {}
    for mi, si in streams:
        p = jnp.exp2(s_ref[mi, si] - ms[mi, si])
        l = jnp.sum(p, axis=0, keepdims=True)
        p_ref[mi, si] = p.astype(jnp.bfloat16)
        acc = jnp.dot(vt, p_ref[mi, si], preferred_element_type=jnp.float32)
        outs[mi, si] = acc / l

    dl = dl_ref[...]
    s1 = jnp.sum(dl[0:1] * dl[1:2], axis=-1, keepdims=True)
    s2 = jnp.sum(dl[2:3] * dl[3:4], axis=-1, keepdims=True)
    lam = jnp.exp(s1) - jnp.exp(s2) + lam_init
    for si in range(n_sub):
        ot = outs[0, si] - lam * outs[1, si]
        ot = ot * lax.rsqrt(jnp.mean(ot * ot, axis=0, keepdims=True) + EPS)
        o = ot.T
        o_ref[0, si * ts:(si + 1) * ts, :] = (o * g_ref[0] * (1.0 - lam_init)).astype(o_ref.dtype)


def diff_attention(qt, k, vt, diff_lambda, subln_g, lam_init, tq=512, ts=256):
    b, h, _, _, t_q = qt.shape
    t_k = k.shape[2]
    tq = min(tq, t_q)
    ts = min(ts, tq)
    n_sub = tq // ts
    kern = functools.partial(_attn_kernel, lam_init=lam_init, n_sub=n_sub)
    return pl.pallas_call(
        kern,
        out_shape=jax.ShapeDtypeStruct((b, t_q, h * 2 * HEAD_DIM), jnp.float32),
        grid=(b, h, t_q // tq),
        in_specs=[pl.BlockSpec((1, 1, 2, 2 * HEAD_DIM, tq), lambda bi, hi, i: (bi, hi, 0, 0, i)),
                  pl.BlockSpec((1, 1, t_k, 2 * HEAD_DIM), lambda bi, hi, i: (bi, hi, 0, 0)),
                  pl.BlockSpec((1, 1, 2 * HEAD_DIM, t_k), lambda bi, hi, i: (bi, hi, 0, 0)),
                  pl.BlockSpec((4, HEAD_DIM), lambda bi, hi, i: (0, 0)),
                  pl.BlockSpec((1, 1, 2 * HEAD_DIM), lambda bi, hi, i: (hi, 0, 0))],
        out_specs=pl.BlockSpec((1, tq, 2 * HEAD_DIM), lambda bi, hi, i: (bi, i, hi)),
        scratch_shapes=[pltpu.VMEM((2, n_sub, t_k, ts), jnp.float32),
                        pltpu.VMEM((2, n_sub, t_k, ts), jnp.bfloat16)],
        compiler_params=_params(3),
        name="diff_attn",
    )(qt, k, vt, diff_lambda.astype(jnp.float32),
      subln_g.astype(jnp.float32).reshape(h, 1, 2 * HEAD_DIM))


@functools.lru_cache(maxsize=None)
def _dft_factors(length):
    n = 2 * length
    fb = min(FREQ_BLOCK, length)
    nb = length // fb
    kk = np.arange(length, dtype=np.int64).reshape(nb, 1, fb)
    k_rows = np.concatenate([kk, kk], axis=1)
    is_sin = np.zeros((nb, 2, fb), bool)
    is_sin[:, 1, :] = True
    k_rows[0, 1, 0] = length
    is_sin[0, 1, 0] = False
    k_rows = k_rows.reshape(-1)
    is_sin = is_sin.reshape(-1)
    n_hi = max(length // LANE, 1)
    n_lo = min(LANE, length)
    a_lo = 2.0 * np.pi * ((k_rows[:, None] * np.arange(n_lo)[None, :]) % n) / n
    a_hi = 2.0 * np.pi * ((k_rows[:, None] * (LANE * np.arange(n_hi))[None, :]) % n) / n
    c_lo, s_lo, c_hi, s_hi = np.cos(a_lo), np.sin(a_lo), np.cos(a_hi), np.sin(a_hi)
    p = np.where(is_sin[:, None], s_hi, c_hi)
    q = np.where(is_sin[:, None], c_hi, -s_hi)
    scale = np.full((2 * length,), 2.0 / n)
    scale[0] = 1.0 / n
    scale[fb] = 1.0 / n
    f32 = lambda a: np.asarray(a, np.float32)
    return f32(p), f32(q), f32(c_lo), f32(s_lo), f32(scale.reshape(-1, 1))


def dft_tables(length):
    p, q, c_lo, s_lo, _ = _dft_factors(length)
    fwd = (jnp.asarray(p)[:, :, None] * jnp.asarray(c_lo)[:, None, :]
           + jnp.asarray(q)[:, :, None] * jnp.asarray(s_lo)[:, None, :])
    fwd = fwd.reshape(2 * length, length).astype(jnp.bfloat16)
    return fwd, fwd.T


def _cmul(za, zb, ha, hb, first_rows):
    ya = za * ha - zb * hb
    yb = za * hb + zb * ha
    ya = jnp.where(first_rows, za * ha, ya)
    yb = jnp.where(first_rows, zb * hb, yb)
    return ya, yb


def _fwd_kernel(f_ref, z_ref, h_ref, c_ref, y_ref, zb_ref, *, fb):
    @pl.when(pl.program_id(1) == 0)
    def _():
        zb_ref[...] = z_ref[0].astype(jnp.bfloat16)

    spec = jnp.dot(f_ref[...], zb_ref[...], preferred_element_type=jnp.float32)
    za, zb = spec[:fb], spec[fb:]
    h = h_ref[...]
    rows = lax.broadcasted_iota(jnp.int32, za.shape, 0)
    first = jnp.logical_and(rows == 0, pl.program_id(1) == 0)
    ya, yb = _cmul(za, zb, h[:fb], h[fb:], first)
    c = c_ref[...]
    y_ref[0, :fb, :] = (ya * c[:fb]).astype(y_ref.dtype)
    y_ref[0, fb:, :] = (yb * c[fb:]).astype(y_ref.dtype)


def hyena_forward(z, hspec, order, fwd, length):
    b, _, w = z.shape
    scale = jnp.asarray(_dft_factors(length)[4])
    fb = min(FREQ_BLOCK, length)
    kern = functools.partial(_fwd_kernel, fb=fb)
    return pl.pallas_call(
        kern,
        out_shape=jax.ShapeDtypeStruct((b, 2 * length, w), jnp.bfloat16),
        grid=(b, length // fb),
        in_specs=[pl.BlockSpec((2 * fb, length), lambda bi, i: (i, 0)),
                  pl.BlockSpec((1, length, w), lambda bi, i: (bi, 0, 0)),
                  pl.BlockSpec((2 * fb, w), lambda bi, i: (i, order)),
                  pl.BlockSpec((2 * fb, 1), lambda bi, i: (i, 0))],
        out_specs=pl.BlockSpec((1, 2 * fb, w), lambda bi, i: (bi, i, 0)),
        scratch_shapes=[pltpu.VMEM((length, w), jnp.bfloat16)],
        compiler_params=_params(2),
        name="hyena_fwd",
    )(fwd, z, hspec, scale)


def _inv_kernel(g_ref, y_ref, z_ref, gate_ref, skip_ref, o_ref):
    conv = jnp.dot(g_ref[...], y_ref[0], preferred_element_type=jnp.float32)
    o_ref[0] = (gate_ref[0] * (conv + z_ref[0] * skip_ref[...])).astype(o_ref.dtype)


def hyena_inverse(y, z, gate, skip, inv, length):
    b, _, w = z.shape
    tm = min(512, length)
    return pl.pallas_call(
        _inv_kernel,
        out_shape=jax.ShapeDtypeStruct((b, length, w), jnp.float32),
        grid=(b, length // tm),
        in_specs=[pl.BlockSpec((tm, 2 * length), lambda bi, i: (i, 0)),
                  pl.BlockSpec((1, 2 * length, w), lambda bi, i: (bi, 0, 0)),
                  pl.BlockSpec((1, tm, w), lambda bi, i: (bi, i, 0)),
                  pl.BlockSpec((1, tm, w), lambda bi, i: (bi, i, 0)),
                  pl.BlockSpec((1, w), lambda bi, i: (0, 0))],
        out_specs=pl.BlockSpec((1, tm, w), lambda bi, i: (bi, i, 0)),
        compiler_params=_params(2),
        name="hyena_inv",
    )(inv, y, z, gate, skip.reshape(1, w))


def _filt_kernel(feat_ref, w1_ref, b1_ref, w2_ref, b2_ref, w3f_ref, w3b_ref, dl_ref,
                 sum_ref, diff_ref, hf_ref, hb_ref, *, rb):
    hi = lax.Precision.HIGHEST
    length, ct = hf_ref.shape
    n_blocks = length // rb

    def taps(i, norm):
        r0 = pl.multiple_of(i * rb, rb)
        feats = feat_ref[pl.ds(r0, rb), :]
        h = jnp.sin(jnp.dot(feats, w1_ref[...], precision=hi, preferred_element_type=jnp.float32)
                    + b1_ref[...])
        h = jnp.sin(jnp.dot(h, w2_ref[...], precision=hi, preferred_element_type=jnp.float32)
                    + b2_ref[...])
        decay = jnp.exp(-feats[:, 0:1] * dl_ref[...])
        hf = jnp.dot(h, w3f_ref[...], precision=hi, preferred_element_type=jnp.float32) * decay
        hb = jnp.dot(h, w3b_ref[...], precision=hi, preferred_element_type=jnp.float32) * decay
        rows = lax.broadcasted_iota(jnp.int32, hb.shape, 0) + r0
        hb = jnp.where(rows == 0, 0.0, hb)
        hf_ref[pl.ds(r0, rb), :] = hf
        hb_ref[pl.ds(r0, rb), :] = hb
        return (norm + jnp.sum(jnp.abs(hf), axis=0, keepdims=True)
                + jnp.sum(jnp.abs(hb), axis=0, keepdims=True))

    norm = lax.fori_loop(0, n_blocks, taps, jnp.zeros((1, ct), jnp.float32))
    inv = 1.0 / norm

    def normalise(i, carry):
        r0 = pl.multiple_of(i * rb, rb)
        hf = hf_ref[pl.ds(r0, rb), :]
        hb = hb_ref[pl.ds(r0, rb), :]
        sum_ref[pl.ds(r0, rb), :] = ((hf + hb) * inv).astype(sum_ref.dtype)
        diff_ref[pl.ds(r0, rb), :] = ((hf - hb) * inv).astype(diff_ref.dtype)
        return carry

    lax.fori_loop(0, n_blocks, normalise, 0)


def hyena_filter_taps(length, filt_w1, filt_b1, filt_w2, filt_b2, filt_w3):
    w = W_HYENA
    ct = 256
    pos = np.arange(length, dtype=np.float64)
    t = pos / max(length - 1, 1)
    bands = np.linspace(1e-4, HYENA_BANDS - 1, HYENA_BANDS)
    ang = (2.0 * math.pi * pos / length)[:, None] * bands[None, :]
    feats = np.concatenate([t[:, None], np.cos(ang), -np.sin(ang)], axis=-1).astype(np.float32)
    deltas = np.abs(np.linspace(math.log(HYENA_TARGET) / HYENA_SLOW_PCT,
                                math.log(HYENA_TARGET) / HYENA_FAST_PCT, w)).astype(np.float32)
    emb, nf = filt_w1.shape
    npc = w // ct
    full = lambda shape: pl.BlockSpec(shape, lambda o, c: (0,) * len(shape))
    kern = functools.partial(_filt_kernel, rb=min(256, length))
    return pl.pallas_call(
        kern,
        out_shape=(jax.ShapeDtypeStruct((length, HYENA_ORDER * w), jnp.bfloat16),
                   jax.ShapeDtypeStruct((length, HYENA_ORDER * w), jnp.bfloat16)),
        grid=(HYENA_ORDER, npc),
        in_specs=[full((length, emb)), full((emb, nf)), full((1, nf)), full((nf, nf)), full((1, nf)),
                  pl.BlockSpec((nf, ct), lambda o, c: (0, o * 2 * npc + c)),
                  pl.BlockSpec((nf, ct), lambda o, c: (0, o * 2 * npc + npc + c)),
                  pl.BlockSpec((1, ct), lambda o, c: (0, c))],
        out_specs=(pl.BlockSpec((length, ct), lambda o, c: (0, o * npc + c)),
                   pl.BlockSpec((length, ct), lambda o, c: (0, o * npc + c))),
        scratch_shapes=[pltpu.VMEM((length, ct), jnp.float32),
                        pltpu.VMEM((length, ct), jnp.float32)],
        compiler_params=_params(2),
        name="hyena_filt",
    )(jnp.asarray(feats), filt_w1, filt_b1.reshape(1, nf), filt_w2, filt_b2.reshape(1, nf),
      filt_w3, filt_w3, jnp.asarray(deltas).reshape(1, w))


def _hspec_kernel(f_ref, hs_ref, hd_ref, o_ref, *, fb):
    f = f_ref[...]
    a = jnp.dot(f, hs_ref[...], preferred_element_type=jnp.float32)
    bd = jnp.dot(f[fb:], hd_ref[...], preferred_element_type=jnp.float32)
    rows = lax.broadcasted_iota(jnp.int32, bd.shape, 0)
    first = jnp.logical_and(rows == 0, pl.program_id(1) == 0)
    o_ref[:fb, :] = a[:fb]
    o_ref[fb:, :] = jnp.where(first, a[fb:], bd)


def hyena_filter_spectrum(hsum, hdiff, fwd, length):
    fb = min(FREQ_BLOCK, length)
    cols = hsum.shape[1]
    w = W_HYENA
    kern = functools.partial(_hspec_kernel, fb=fb)
    return pl.pallas_call(
        kern,
        out_shape=jax.ShapeDtypeStruct((2 * length, cols), jnp.float32),
        grid=(cols // w, length // fb),
        in_specs=[pl.BlockSpec((2 * fb, length), lambda j, i: (i, 0)),
                  pl.BlockSpec((length, w), lambda j, i: (0, j)),
                  pl.BlockSpec((length, w), lambda j, i: (0, j))],
        out_specs=pl.BlockSpec((2 * fb, w), lambda j, i: (i, j)),
        compiler_params=_params(2),
        name="hyena_hspec",
    )(fwd, hsum, hdiff)


def hyena_long_conv(v, x1, x2, hyena_skip, filt):
    length = v.shape[1]
    fwd, inv = dft_tables(length)
    hsum, hdiff = hyena_filter_taps(length, *filt)
    hspec = hyena_filter_spectrum(hsum, hdiff, fwd, length)
    z = v
    for o, gate in enumerate((x1, x2)):
        y = hyena_forward(z, hspec, o, fwd, length)
        z = hyena_inverse(y, z, gate, hyena_skip[o], inv, length)
    return z


def rms_norm(x, g):
    y = x * lax.rsqrt(jnp.mean(x * x, axis=-1, keepdims=True) + EPS)
    return y * g


def layer_norm(x, g, b):
    mu = jnp.mean(x, axis=-1, keepdims=True)
    var = jnp.mean(jnp.square(x - mu), axis=-1, keepdims=True)
    return (x - mu) * lax.rsqrt(var + EPS) * g + b


def depthwise_conv(x, w, b):
    k = w.shape[0]
    pad = (k - 1) // 2
    y = lax.conv_general_dilated(x, w[:, None, :], (1,), [(pad, pad)],
                                 dimension_numbers=('NWC', 'WIO', 'NWC'),
                                 feature_group_count=x.shape[-1])
    return y + b


def axial_rope(rows):
    row = jnp.repeat(jnp.arange(rows, dtype=jnp.int32), GRID_W).astype(jnp.float32)
    col = jnp.tile(jnp.arange(GRID_W, dtype=jnp.int32), rows).astype(jnp.float32)
    n_freq = HEAD_DIM // 4
    inv = ROPE_BASE ** (-jnp.arange(n_freq, dtype=jnp.float32) / n_freq)
    ang = jnp.concatenate([row[:, None] * inv, col[:, None] * inv], axis=-1)
    return jnp.cos(ang), jnp.sin(ang)


def apply_rope(x, cos, sin):
    half = HEAD_DIM // 2
    x1 = x[..., :half]
    x2 = x[..., half:]
    c = cos[:, None, None, :]
    s = sin[:, None, None, :]
    return jnp.concatenate([x1 * c - x2 * s, x1 * s + x2 * c], axis=-1)


def mm3(x, w, name):
    b, t, k = x.shape
    return matmul(x.reshape(b * t, k), w, name=name).reshape(b, t, w.shape[1])


def conformer_conv(u, p):
    a, gate = jnp.split(u, 2, axis=-1)
    y = a * jax.nn.sigmoid(gate)
    y = depthwise_conv(y, p['conv_a_w'], p['conv_a_b'])
    y = jax.nn.silu(layer_norm(y, p['ln_a_g'], p['ln_a_b']))
    return mm3(y, p['w_a_out'], "mm_a_out")


def hyena_branch(u, p):
    u = depthwise_conv(u, p['short_b_w'], p['short_b_b'])
    v, x1, x2 = jnp.split(u, 3, axis=-1)
    filt = (p['filt_w1'], p['filt_b1'], p['filt_w2'], p['filt_b2'], p['filt_w3'])
    z = hyena_long_conv(v, x1, x2, p['hyena_skip'], filt)
    return mm3(z, p['w_b_out'], "mm_b_out")


def rope_heads(t, cos, sin):
    b, tt, _ = t.shape
    t = apply_rope(t.reshape(b, tt, N_HEADS, 2, HEAD_DIM), cos, sin)
    return t.reshape(b, tt, N_HEADS * 2 * HEAD_DIM)


def query_layout(q):
    b, tt, _ = q.shape
    qt = q.reshape(b, tt, N_HEADS, 2 * HEAD_DIM).transpose(0, 2, 3, 1)
    z = jnp.zeros_like(qt[:, :, :HEAD_DIM])
    return jnp.stack([jnp.concatenate([qt[:, :, :HEAD_DIM], z], axis=2),
                      jnp.concatenate([z, qt[:, :, HEAD_DIM:]], axis=2)], axis=2)


def key_layout(k):
    b, tt, _ = k.shape
    return k.reshape(b, tt, N_HEADS, 2 * HEAD_DIM).transpose(0, 2, 1, 3)


def value_layout(v):
    b, tt, _ = v.shape
    return v.reshape(b, tt, N_HEADS, 2 * HEAD_DIM).transpose(0, 2, 3, 1)


def gated_merge(br_a, br_b, br_c, g_logits, p):
    g = jax.nn.sigmoid(g_logits + p['b_gate'])
    g_a, g_b, g_c = jnp.split(g, N_BRANCH, axis=-1)
    return mm3(g_a * br_a + g_b * br_b + g_c * br_c, p['w_o'], "mm_o")


def token_mixers(hx, hc, p, rope_cos, rope_sin, lam_init, update_ctx):
    cuts = [2 * W_CONV,
            2 * W_CONV + 3 * W_HYENA,
            2 * W_CONV + 3 * W_HYENA + W_DIFF,
            2 * W_CONV + 3 * W_HYENA + 2 * W_DIFF,
            2 * W_CONV + 3 * W_HYENA + 3 * W_DIFF]
    ax, bx, qx, kx, vx, gx = jnp.split(mm3(hx, p['w_in'], "mm_in_x"), cuts, axis=-1)
    ac, bc, qc, kc, vc, gc = jnp.split(mm3(hc, p['w_in'], "mm_in_c"), cuts, axis=-1)
    qscale = HEAD_DIM ** -0.5 * LOG2E
    bf = jnp.bfloat16
    qx = query_layout((rope_heads(qx, rope_cos, rope_sin) * qscale).astype(bf))
    k_all = jnp.concatenate([kc, rope_heads(kx, rope_cos, rope_sin)], axis=1).astype(bf)
    v_all = jnp.concatenate([vc, vx], axis=1).astype(bf)
    ox = diff_attention(qx, key_layout(k_all), value_layout(v_all),
                        p['diff_lambda'], p['subln_g'], lam_init)
    br_c = mm3(ox, p['w_c_out'], "mm_c_out_x")
    out_x = gated_merge(conformer_conv(ax, p), hyena_branch(bx, p), br_c, gx, p)
    if not update_ctx:
        return out_x, None
    oc = diff_attention(query_layout((qc * qscale).astype(bf)), key_layout(kc.astype(bf)),
                        value_layout(vc.astype(bf)), p['diff_lambda'], p['subln_g'], lam_init)
    br_cc = mm3(oc, p['w_c_out'], "mm_c_out_c")
    out_c = gated_merge(conformer_conv(ac, p), hyena_branch(bc, p), br_cc, gc, p)
    return out_x, out_c


def conv_ffn(h, p):
    u = depthwise_conv(mm3(h, p['w_up'], "mm_up"), p['conv_f_w'], p['conv_f_b'])
    a, v = jnp.split(u, 2, axis=-1)
    return mm3(jax.nn.silu(a) * v, p['w_down'], "mm_down")


def kernel(x, c, ctx, c_ctx, w_ada, b_ada, norm1_g, norm2_g, w_in, b_gate, conv_a_w, conv_a_b, ln_a_g, ln_a_b, w_a_out, short_b_w, short_b_b, filt_w1, filt_b1, filt_w2, filt_b2, filt_w3, hyena_skip, w_b_out, diff_lambda, subln_g, w_c_out, w_o, w_up, conv_f_w, conv_f_b, w_down, final_g):
    rows = x.shape[1] // GRID_W
    rope_cos, rope_sin = axial_rope(rows)
    hi = lax.Precision.HIGHEST
    for l in range(DEPTH):
        p = {
            'w_in': w_in[l], 'b_gate': b_gate[l],
            'conv_a_w': conv_a_w[l], 'conv_a_b': conv_a_b[l],
            'ln_a_g': ln_a_g[l], 'ln_a_b': ln_a_b[l], 'w_a_out': w_a_out[l],
            'short_b_w': short_b_w[l], 'short_b_b': short_b_b[l],
            'filt_w1': filt_w1[l], 'filt_b1': filt_b1[l], 'filt_w2': filt_w2[l],
            'filt_b2': filt_b2[l], 'filt_w3': filt_w3[l],
            'hyena_skip': hyena_skip[l], 'w_b_out': w_b_out[l],
            'diff_lambda': diff_lambda[l], 'subln_g': subln_g[l], 'w_c_out': w_c_out[l],
            'w_o': w_o[l], 'w_up': w_up[l], 'conv_f_w': conv_f_w[l],
            'conv_f_b': conv_f_b[l], 'w_down': w_down[l],
        }
        update_ctx = l < DEPTH - 1
        lam_init = 0.8 - 0.6 * math.exp(-0.3 * l)
        cc = jnp.concatenate([c, c_ctx[None, :]], axis=0)
        mod = jnp.dot(jax.nn.silu(cc), w_ada[l], precision=hi) + b_ada[l]
        mx = jnp.split(mod[:-1, None, :], 6, axis=-1)
        mc = jnp.split(mod[-1:, None, :], 6, axis=-1)
        hx = rms_norm(x, norm1_g[l]) * (1 + mx[1]) + mx[0]
        hc = rms_norm(ctx, norm1_g[l]) * (1 + mc[1]) + mc[0]
        out_x, out_c = token_mixers(hx, hc, p, rope_cos, rope_sin, lam_init, update_ctx)
        x = x + mx[2] * out_x
        hx = rms_norm(x, norm2_g[l]) * (1 + mx[4]) + mx[3]
        x = x + mx[5] * conv_ffn(hx, p)
        if update_ctx:
            ctx = ctx + mc[2] * out_c
            hc = rms_norm(ctx, norm2_g[l]) * (1 + mc[4]) + mc[3]
            ctx = ctx + mc[5] * conv_ffn(hc, p)
    return rms_norm(x, final_g)
```

```python
import functools
import math

import numpy as np
import jax
import jax.numpy as jnp
from jax import lax
from jax.experimental import pallas as pl
from jax.experimental.pallas import tpu as pltpu

D_MODEL = 2048
DEPTH = 2
GRID_W = 64
W_CONV = D_MODEL // 4
W_HYENA = D_MODEL // 4
N_HEADS = 8
HEAD_DIM = 64
HEAD_W = 2 * HEAD_DIM
W_DIFF = N_HEADS * HEAD_W
D_FF = 5632
HYENA_ORDER = 2
HYENA_BANDS = 8
HYENA_TARGET = 1e-2
HYENA_FAST_PCT = 0.3
HYENA_SLOW_PCT = 1.5
ROPE_BASE = 10000.0
EPS = 1e-6
LOG2E = 1.4426950408889634

LANE = 128
HALO = 16
FREQ_BLOCK = 256
TN = 512
VMEM_LIMIT_BYTES = 56 * 1024 * 1024

N_IN = 2 * W_CONV + 3 * W_HYENA + 3 * W_DIFF + 3 * D_MODEL
N_MIX = N_IN - 3 * D_MODEL
COL_GATE = 0
COL_A = 3 * D_MODEL
COL_B = COL_A + 2 * W_CONV
COL_Q = COL_B + 3 * W_HYENA
COL_K = COL_Q + W_DIFF
COL_V = COL_K + W_DIFF

F32 = jnp.float32
BF16 = jnp.bfloat16


def _params(n_axes):
    return pltpu.CompilerParams(
        dimension_semantics=("arbitrary",) * n_axes, vmem_limit_bytes=VMEM_LIMIT_BYTES)


def _rms(x, g):
    return x * lax.rsqrt(jnp.mean(x * x, axis=-1, keepdims=True) + EPS) * g


def _ada_kernel(c_ref, w_ref, b_ref, o_ref):
    c = c_ref[...]
    h = c * jax.nn.sigmoid(c)
    o_ref[...] = jnp.dot(h, w_ref[...], precision=lax.Precision.HIGHEST,
                         preferred_element_type=F32) + b_ref[...]


def ada_projection(cc, w, b):
    m, d = cc.shape
    n = w.shape[1]
    tn = 1024
    return pl.pallas_call(
        _ada_kernel,
        out_shape=jax.ShapeDtypeStruct((m, n), F32),
        grid=(n // tn,),
        in_specs=[pl.BlockSpec((m, d), lambda j: (0, 0)),
                  pl.BlockSpec((d, tn), lambda j: (0, j)),
                  pl.BlockSpec((1, tn), lambda j: (0, j))],
        out_specs=pl.BlockSpec((m, tn), lambda j: (0, j)),
        compiler_params=_params(1),
        name="ada_proj",
    )(cc, w, b.reshape(1, n))


def _nmm_kernel(x_ref, g_ref, sh_ref, sc_ref, w_ref, o_ref, h_ref):
    @pl.when(pl.program_id(2) == 0)
    def _():
        y = _rms(x_ref[0], g_ref[...])
        h_ref[...] = (y * (1.0 + sc_ref[0]) + sh_ref[0]).astype(BF16)

    o_ref[0] = jnp.dot(h_ref[...], w_ref[...].astype(BF16),
                       preferred_element_type=F32).astype(o_ref.dtype)


def norm_mod_matmul(x, g, shift, scale, w, col_rot, name):
    b, t, d = x.shape
    n = w.shape[1]
    tm = min(1024, t)
    nt = n // TN
    per_batch = shift.shape[0] > 1
    mod_spec = pl.BlockSpec((1, 1, d), (lambda bi, i, j: (bi, 0, 0)) if per_batch
                            else (lambda bi, i, j: (0, 0, 0)))
    return pl.pallas_call(
        _nmm_kernel,
        out_shape=jax.ShapeDtypeStruct((b, t, n), BF16),
        grid=(b, t // tm, nt),
        in_specs=[pl.BlockSpec((1, tm, d), lambda bi, i, j: (bi, i, 0)),
                  pl.BlockSpec((1, d), lambda bi, i, j: (0, 0)),
                  mod_spec, mod_spec,
                  pl.BlockSpec((d, TN), lambda bi, i, j: (0, (j + col_rot) % nt))],
        out_specs=pl.BlockSpec((1, tm, TN), lambda bi, i, j: (bi, i, j)),
        scratch_shapes=[pltpu.VMEM((tm, d), BF16)],
        compiler_params=_params(3),
        name=name,
    )(x, g.reshape(1, d), shift, scale, w)


def _dwconv_kernel(*refs, taps, glu, post, tt, rb):
    n_in = 2 if glu else 1
    main = refs[:n_in]
    prev = refs[n_in:2 * n_in]
    nxt = refs[2 * n_in:3 * n_in]
    rest = refs[3 * n_in:]
    if post:
        w_ref, b_ref, lg_ref, lb_ref, o_ref, ext_ref = rest
    else:
        w_ref, b_ref, o_ref, ext_ref = rest
    i = pl.program_id(1)
    last = pl.num_programs(1) - 1

    def value(blocks):
        a = blocks[0][0].astype(F32)
        if glu:
            a = a * jax.nn.sigmoid(blocks[1][0].astype(F32))
        return a

    ext_ref[0:HALO, :] = jnp.where(i == 0, 0.0, value(prev))
    ext_ref[HALO:HALO + tt, :] = value(main)
    ext_ref[HALO + tt:2 * HALO + tt, :] = jnp.where(i == last, 0.0, value(nxt))

    pad = (taps - 1) // 2
    for r in range(tt // rb):
        acc = jnp.zeros((rb, ext_ref.shape[1]), F32) + b_ref[...]
        for j in range(taps):
            acc = acc + w_ref[j:j + 1, :] * ext_ref[pl.ds(HALO - pad + j + r * rb, rb), :]
        if post:
            mu = jnp.mean(acc, axis=-1, keepdims=True)
            cen = acc - mu
            var = jnp.mean(cen * cen, axis=-1, keepdims=True)
            y = cen * lax.rsqrt(var + EPS) * lg_ref[...] + lb_ref[...]
            acc = y * jax.nn.sigmoid(y)
        o_ref[0, r * rb:(r + 1) * rb, :] = acc.astype(o_ref.dtype)


def depthwise_conv(y, col, width, w, bias, glu=False, ln=None, out_dtype=F32, name="dwconv"):
    b, t, _ = y.shape
    taps = w.shape[0]
    tt = min(512, t)
    cw = width if ln is not None else min(512, width)
    ncb = width // cw
    cb0 = col // cw
    hb = tt // HALO
    n_hb = t // HALO
    cols = [cb0, cb0 + ncb] if glu else [cb0]

    def main_spec(c0):
        return pl.BlockSpec((1, tt, cw), lambda bi, i, c: (bi, i, c0 + c))

    def prev_spec(c0):
        return pl.BlockSpec((1, HALO, cw), lambda bi, i, c: (bi, jnp.maximum(i * hb - 1, 0), c0 + c))

    def next_spec(c0):
        return pl.BlockSpec((1, HALO, cw),
                            lambda bi, i, c: (bi, jnp.minimum((i + 1) * hb, n_hb - 1), c0 + c))

    vec = lambda rows: pl.BlockSpec((rows, cw), lambda bi, i, c: (0, c))
    in_specs = ([main_spec(c0) for c0 in cols] + [prev_spec(c0) for c0 in cols]
                + [next_spec(c0) for c0 in cols] + [vec(taps), vec(1)])
    args = [y] * (3 * len(cols)) + [w, bias.reshape(1, width)]
    if ln is not None:
        in_specs += [vec(1), vec(1)]
        args += [ln[0].reshape(1, width), ln[1].reshape(1, width)]
    kern = functools.partial(_dwconv_kernel, taps=taps, glu=glu, post=ln is not None, tt=tt,
                             rb=min(128, tt))
    return pl.pallas_call(
        kern,
        out_shape=jax.ShapeDtypeStruct((b, t, width), out_dtype),
        grid=(b, t // tt, ncb),
        in_specs=in_specs,
        out_specs=pl.BlockSpec((1, tt, cw), lambda bi, i, c: (bi, i, c)),
        scratch_shapes=[pltpu.VMEM((tt + 2 * HALO, cw), F32)],
        compiler_params=_params(3),
        name=name,
    )(*args)


def _swap_halves(x):
    lanes = lax.broadcasted_iota(jnp.int32, x.shape, 1)
    first = (lanes % HEAD_DIM) < (HEAD_DIM // 2)
    return jnp.where(first, pltpu.roll(x, LANE - HEAD_DIM // 2, 1), pltpu.roll(x, HEAD_DIM // 2, 1))


def _attn_kernel(*refs, lam_init, n_sub, t_c, has_lat):
    if has_lat:
        (q_ref, kc_ref, vc_ref, kx_ref, vx_ref, cq_ref, sq_ref, ck_ref, sk_ref, dl_ref, g_ref,
         o_ref, s_ref, p_ref, k_s, vt_s) = refs
    else:
        q_ref, kc_ref, vc_ref, dl_ref, g_ref, o_ref, s_ref, p_ref, k_s, vt_s = refs
    t_k = k_s.shape[0]
    tq = q_ref.shape[1]
    ts = tq // n_sub
    ck = 256

    @pl.when(pl.program_id(2) == 0)
    def _():
        k_s[0:t_c, :] = kc_ref[0]
        if has_lat:
            kx = kx_ref[0].astype(F32)
            k_s[t_c:, :] = (kx * ck_ref[...] + _swap_halves(kx) * sk_ref[...]).astype(BF16)
        for c in range(t_k // ck):
            lo = c * ck
            src = vc_ref[0, lo:lo + ck, :] if lo < t_c else vx_ref[0, lo - t_c:lo - t_c + ck, :]
            vt_s[:, lo:lo + ck] = src.astype(F32).T.astype(BF16)

    q = q_ref[0].astype(F32)
    if has_lat:
        q = q * cq_ref[...] + _swap_halves(q) * sq_ref[...]
    qt = (q * (HEAD_DIM ** -0.5 * LOG2E)).T
    rows = lax.broadcasted_iota(jnp.int32, qt.shape, 0)
    qts = (jnp.where(rows < HEAD_DIM, qt, 0.0).astype(BF16),
           jnp.where(rows >= HEAD_DIM, qt, 0.0).astype(BF16))

    kk = k_s[...]
    vt = vt_s[...]
    streams = [(mi, si) for si in range(n_sub) for mi in range(2)]
    ms = {}
    for mi, si in streams:
        s = jnp.dot(kk, qts[mi][:, si * ts:(si + 1) * ts], preferred_element_type=F32)
        s_ref[mi, si] = s
        ms[mi, si] = jnp.max(s, axis=0, keepdims=True)
    outs = {}
    for mi, si in streams:
        p = jnp.exp2(s_ref[mi, si] - ms[mi, si])
        l = jnp.sum(p, axis=0, keepdims=True)
        p_ref[mi, si] = p.astype(BF16)
        acc = jnp.dot(vt, p_ref[mi, si], preferred_element_type=F32)
        outs[mi, si] = acc / l

    dl = dl_ref[...]
    s1 = jnp.sum(dl[0:1] * dl[1:2], axis=-1, keepdims=True)
    s2 = jnp.sum(dl[2:3] * dl[3:4], axis=-1, keepdims=True)
    lam = jnp.exp(s1) - jnp.exp(s2) + lam_init
    for si in range(n_sub):
        ot = outs[0, si] - lam * outs[1, si]
        ot = ot * lax.rsqrt(jnp.mean(ot * ot, axis=0, keepdims=True) + EPS)
        o = ot.T
        o_ref[0, si * ts:(si + 1) * ts, :] = (o * g_ref[0] * (1.0 - lam_init)).astype(o_ref.dtype)


def diff_attention(y_q, y_c, y_x, rope, diff_lambda, subln_g, lam_init):
    b, t_q, _ = y_q.shape
    t_c = y_c.shape[1]
    has_lat = y_x is not None
    t_k = t_c + (y_x.shape[1] if has_lat else 0)
    tq = min(512, t_q)
    ts = min(256, tq)
    n_sub = tq // ts
    qb, kb, vb = COL_Q // HEAD_W, COL_K // HEAD_W, COL_V // HEAD_W
    head = lambda base, rows: pl.BlockSpec((1, rows, HEAD_W), lambda bi, hi, i: (bi, 0, base + hi))
    in_specs = [pl.BlockSpec((1, tq, HEAD_W), lambda bi, hi, i: (bi, i, qb + hi)),
                head(kb, t_c), head(vb, t_c)]
    args = [y_q, y_c, y_c]
    if has_lat:
        t_x = y_x.shape[1]
        in_specs += [head(kb, t_x), head(vb, t_x),
                     pl.BlockSpec((tq, HEAD_W), lambda bi, hi, i: (i, 0)),
                     pl.BlockSpec((tq, HEAD_W), lambda bi, hi, i: (i, 0)),
                     pl.BlockSpec((t_x, HEAD_W), lambda bi, hi, i: (0, 0)),
                     pl.BlockSpec((t_x, HEAD_W), lambda bi, hi, i: (0, 0))]
        args += [y_x, y_x, rope[0], rope[1], rope[0], rope[1]]
    in_specs += [pl.BlockSpec((4, HEAD_DIM), lambda bi, hi, i: (0, 0)),
                 pl.BlockSpec((1, 1, HEAD_W), lambda bi, hi, i: (hi, 0, 0))]
    args += [diff_lambda.astype(F32), subln_g.astype(F32).reshape(N_HEADS, 1, HEAD_W)]
    kern = functools.partial(_attn_kernel, lam_init=lam_init, n_sub=n_sub, t_c=t_c, has_lat=has_lat)
    return pl.pallas_call(
        kern,
        out_shape=jax.ShapeDtypeStruct((b, t_q, W_DIFF), BF16),
        grid=(b, N_HEADS, t_q // tq),
        in_specs=in_specs,
        out_specs=pl.BlockSpec((1, tq, HEAD_W), lambda bi, hi, i: (bi, i, hi)),
        scratch_shapes=[pltpu.VMEM((2, n_sub, t_k, ts), F32),
                        pltpu.VMEM((2, n_sub, t_k, ts), BF16),
                        pltpu.VMEM((t_k, HEAD_W), BF16),
                        pltpu.VMEM((HEAD_W, t_k), BF16)],
        compiler_params=_params(3),
        name="diff_attn",
    )(*args)


def rope_tables(t):
    rows = t // GRID_W
    row = np.repeat(np.arange(rows), GRID_W).astype(np.float64)
    col = np.tile(np.arange(GRID_W), rows).astype(np.float64)
    n_freq = HEAD_DIM // 4
    inv = (ROPE_BASE ** (-np.arange(n_freq, dtype=np.float32) / n_freq)).astype(np.float64)
    ang = np.concatenate([row[:, None] * inv, col[:, None] * inv], axis=-1)
    c, s = np.cos(ang), np.sin(ang)
    cos_t = np.tile(np.concatenate([c, c], axis=-1), (1, 2))
    sin_t = np.tile(np.concatenate([-s, s], axis=-1), (1, 2))
    return jnp.asarray(cos_t, F32), jnp.asarray(sin_t, F32)


@functools.lru_cache(maxsize=None)
def _dft_factors(length):
    n = 2 * length
    fb = min(FREQ_BLOCK, length)
    nb = length // fb
    kk = np.arange(length, dtype=np.int64).reshape(nb, 1, fb)
    k_rows = np.concatenate([kk, kk], axis=1)
    is_sin = np.zeros((nb, 2, fb), bool)
    is_sin[:, 1, :] = True
    k_rows[0, 1, 0] = length
    is_sin[0, 1, 0] = False
    k_rows = k_rows.reshape(-1)
    is_sin = is_sin.reshape(-1)
    n_hi = max(length // LANE, 1)
    n_lo = min(LANE, length)
    a_lo = 2.0 * np.pi * ((k_rows[:, None] * np.arange(n_lo)[None, :]) % n) / n
    a_hi = 2.0 * np.pi * ((k_rows[:, None] * (LANE * np.arange(n_hi))[None, :]) % n) / n
    c_lo, s_lo, c_hi, s_hi = np.cos(a_lo), np.sin(a_lo), np.cos(a_hi), np.sin(a_hi)
    p = np.where(is_sin[:, None], s_hi, c_hi)
    q = np.where(is_sin[:, None], c_hi, -s_hi)
    scale = np.full((2 * length,), 2.0 / n)
    scale[0] = 1.0 / n
    scale[fb] = 1.0 / n
    f32 = lambda a: np.asarray(a, np.float32)
    return f32(p), f32(q), f32(c_lo), f32(s_lo), f32(scale.reshape(-1, 1))


def dft_tables(length):
    p, q, c_lo, s_lo, _ = _dft_factors(length)
    fwd = (jnp.asarray(p)[:, :, None] * jnp.asarray(c_lo)[:, None, :]
           + jnp.asarray(q)[:, :, None] * jnp.asarray(s_lo)[:, None, :])
    fwd = fwd.reshape(2 * length, length).astype(BF16)
    return fwd, fwd.T


def _cmul(za, zb, ha, hb, first_rows):
    ya = za * ha - zb * hb
    yb = za * hb + zb * ha
    ya = jnp.where(first_rows, za * ha, ya)
    yb = jnp.where(first_rows, zb * hb, yb)
    return ya, yb


def _fwd_kernel(f_ref, z_ref, h_ref, c_ref, y_ref, zb_ref, *, fb):
    @pl.when(pl.program_id(1) == 0)
    def _():
        zb_ref[...] = z_ref[0].astype(BF16)

    spec = jnp.dot(f_ref[...], zb_ref[...], preferred_element_type=F32)
    za, zb = spec[:fb], spec[fb:]
    h = h_ref[...]
    rows = lax.broadcasted_iota(jnp.int32, za.shape, 0)
    first = jnp.logical_and(rows == 0, pl.program_id(1) == 0)
    ya, yb = _cmul(za, zb, h[:fb], h[fb:], first)
    c = c_ref[...]
    y_ref[0, :fb, :] = (ya * c[:fb]).astype(y_ref.dtype)
    y_ref[0, fb:, :] = (yb * c[fb:]).astype(y_ref.dtype)


def hyena_forward(z, z_col, hspec, order, fwd, length):
    b = z.shape[0]
    w = W_HYENA
    scale = jnp.asarray(_dft_factors(length)[4])
    fb = min(FREQ_BLOCK, length)
    kern = functools.partial(_fwd_kernel, fb=fb)
    return pl.pallas_call(
        kern,
        out_shape=jax.ShapeDtypeStruct((b, 2 * length, w), BF16),
        grid=(b, length // fb),
        in_specs=[pl.BlockSpec((2 * fb, length), lambda bi, i: (i, 0)),
                  pl.BlockSpec((1, length, w), lambda bi, i: (bi, 0, z_col)),
                  pl.BlockSpec((2 * fb, w), lambda bi, i: (i, order)),
                  pl.BlockSpec((2 * fb, 1), lambda bi, i: (i, 0))],
        out_specs=pl.BlockSpec((1, 2 * fb, w), lambda bi, i: (bi, i, 0)),
        scratch_shapes=[pltpu.VMEM((length, w), BF16)],
        compiler_params=_params(2),
        name="hyena_fwd",
    )(fwd, z, hspec, scale)


def _inv_kernel(g_ref, y_ref, z_ref, gate_ref, skip_ref, o_ref):
    conv = jnp.dot(g_ref[...], y_ref[0], preferred_element_type=F32)
    o_ref[0] = (gate_ref[0] * (conv + z_ref[0] * skip_ref[...])).astype(o_ref.dtype)


def hyena_inverse(y, z, z_col, gate, gate_col, skip, inv, length, out_dtype):
    b = z.shape[0]
    w = W_HYENA
    tm = min(512, length)
    return pl.pallas_call(
        _inv_kernel,
        out_shape=jax.ShapeDtypeStruct((b, length, w), out_dtype),
        grid=(b, length // tm),
        in_specs=[pl.BlockSpec((tm, 2 * length), lambda bi, i: (i, 0)),
                  pl.BlockSpec((1, 2 * length, w), lambda bi, i: (bi, 0, 0)),
                  pl.BlockSpec((1, tm, w), lambda bi, i: (bi, i, z_col)),
                  pl.BlockSpec((1, tm, w), lambda bi, i: (bi, i, gate_col)),
                  pl.BlockSpec((1, w), lambda bi, i: (0, 0))],
        out_specs=pl.BlockSpec((1, tm, w), lambda bi, i: (bi, i, 0)),
        compiler_params=_params(2),
        name="hyena_inv",
    )(inv, y, z, gate, skip.reshape(1, w))


def _filt_kernel(feat_ref, w1_ref, b1_ref, w2_ref, b2_ref, w3f_ref, w3b_ref, dl_ref,
                 sum_ref, diff_ref, hf_ref, hb_ref, *, rb):
    hi = lax.Precision.HIGHEST
    length, ct = hf_ref.shape
    n_blocks = length // rb

    def taps(i, norm):
        r0 = pl.multiple_of(i * rb, rb)
        feats = feat_ref[pl.ds(r0, rb), :]
        h = jnp.sin(jnp.dot(feats, w1_ref[...], precision=hi, preferred_element_type=F32)
                    + b1_ref[...])
        h = jnp.sin(jnp.dot(h, w2_ref[...], precision=hi, preferred_element_type=F32)
                    + b2_ref[...])
        decay = jnp.exp(-feats[:, 0:1] * dl_ref[...])
        hf = jnp.dot(h, w3f_ref[...], precision=hi, preferred_element_type=F32) * decay
        hb = jnp.dot(h, w3b_ref[...], precision=hi, preferred_element_type=F32) * decay
        rows = lax.broadcasted_iota(jnp.int32, hb.shape, 0) + r0
        hb = jnp.where(rows == 0, 0.0, hb)
        hf_ref[pl.ds(r0, rb), :] = hf
        hb_ref[pl.ds(r0, rb), :] = hb
        return (norm + jnp.sum(jnp.abs(hf), axis=0, keepdims=True)
                + jnp.sum(jnp.abs(hb), axis=0, keepdims=True))

    norm = lax.fori_loop(0, n_blocks, taps, jnp.zeros((1, ct), F32))
    inv = 1.0 / norm

    def normalise(i, carry):
        r0 = pl.multiple_of(i * rb, rb)
        hf = hf_ref[pl.ds(r0, rb), :]
        hb = hb_ref[pl.ds(r0, rb), :]
        sum_ref[pl.ds(r0, rb), :] = ((hf + hb) * inv).astype(sum_ref.dtype)
        diff_ref[pl.ds(r0, rb), :] = ((hf - hb) * inv).astype(diff_ref.dtype)
        return carry

    lax.fori_loop(0, n_blocks, normalise, 0)


def hyena_filter_taps(length, filt_w1, filt_b1, filt_w2, filt_b2, filt_w3):
    w = W_HYENA
    ct = 256
    pos = np.arange(length, dtype=np.float64)
    t = pos / max(length - 1, 1)
    bands = np.linspace(1e-4, HYENA_BANDS - 1, HYENA_BANDS)
    ang = (2.0 * math.pi * pos / length)[:, None] * bands[None, :]
    feats = np.concatenate([t[:, None], np.cos(ang), -np.sin(ang)], axis=-1).astype(np.float32)
    deltas = np.abs(np.linspace(math.log(HYENA_TARGET) / HYENA_SLOW_PCT,
                                math.log(HYENA_TARGET) / HYENA_FAST_PCT, w)).astype(np.float32)
    emb, nf = filt_w1.shape
    npc = w // ct
    full = lambda shape: pl.BlockSpec(shape, lambda o, c: (0,) * len(shape))
    kern = functools.partial(_filt_kernel, rb=min(256, length))
    return pl.pallas_call(
        kern,
        out_shape=(jax.ShapeDtypeStruct((length, HYENA_ORDER * w), BF16),
                   jax.ShapeDtypeStruct((length, HYENA_ORDER * w), BF16)),
        grid=(HYENA_ORDER, npc),
        in_specs=[full((length, emb)), full((emb, nf)), full((1, nf)), full((nf, nf)), full((1, nf)),
                  pl.BlockSpec((nf, ct), lambda o, c: (0, o * 2 * npc + c)),
                  pl.BlockSpec((nf, ct), lambda o, c: (0, o * 2 * npc + npc + c)),
                  pl.BlockSpec((1, ct), lambda o, c: (0, c))],
        out_specs=(pl.BlockSpec((length, ct), lambda o, c: (0, o * npc + c)),
                   pl.BlockSpec((length, ct), lambda o, c: (0, o * npc + c))),
        scratch_shapes=[pltpu.VMEM((length, ct), F32),
                        pltpu.VMEM((length, ct), F32)],
        compiler_params=_params(2),
        name="hyena_filt",
    )(jnp.asarray(feats), filt_w1, filt_b1.reshape(1, nf), filt_w2, filt_b2.reshape(1, nf),
      filt_w3, filt_w3, jnp.asarray(deltas).reshape(1, w))


def _hspec_kernel(f_ref, hs_ref, hd_ref, o_ref, *, fb):
    f = f_ref[...]
    a = jnp.dot(f, hs_ref[...], preferred_element_type=F32)
    bd = jnp.dot(f[fb:], hd_ref[...], preferred_element_type=F32)
    rows = lax.broadcasted_iota(jnp.int32, bd.shape, 0)
    first = jnp.logical_and(rows == 0, pl.program_id(1) == 0)
    o_ref[:fb, :] = a[:fb]
    o_ref[fb:, :] = jnp.where(first, a[fb:], bd)


def hyena_filter_spectrum(hsum, hdiff, fwd, length):
    fb = min(FREQ_BLOCK, length)
    cols = hsum.shape[1]
    w = W_HYENA
    kern = functools.partial(_hspec_kernel, fb=fb)
    return pl.pallas_call(
        kern,
        out_shape=jax.ShapeDtypeStruct((2 * length, cols), F32),
        grid=(cols // w, length // fb),
        in_specs=[pl.BlockSpec((2 * fb, length), lambda j, i: (i, 0)),
                  pl.BlockSpec((length, w), lambda j, i: (0, j)),
                  pl.BlockSpec((length, w), lambda j, i: (0, j))],
        out_specs=pl.BlockSpec((2 * fb, w), lambda j, i: (i, j)),
        compiler_params=_params(2),
        name="hyena_hspec",
    )(fwd, hsum, hdiff)


def hyena_long_conv(u, hyena_skip, filt):
    length = u.shape[1]
    fwd, inv = dft_tables(length)
    hsum, hdiff = hyena_filter_taps(length, *filt)
    hspec = hyena_filter_spectrum(hsum, hdiff, fwd, length)
    y = hyena_forward(u, 0, hspec, 0, fwd, length)
    z = hyena_inverse(y, u, 0, u, 1, hyena_skip[0], inv, length, F32)
    y = hyena_forward(z, 0, hspec, 1, fwd, length)
    return hyena_inverse(y, z, 0, u, 2, hyena_skip[1], inv, length, BF16)


def _merge_kernel(ya_ref, zb_ref, oc_ref, ga_ref, gb_ref, gc_ref, bg_ref, wa_ref, wb_ref, wc_ref,
                  wo_ref, x_ref, mg_ref, o_ref):
    d = D_MODEL

    def branch(act_ref, w_ref, g_ref, k):
        br = jnp.dot(act_ref[0], w_ref[...], preferred_element_type=F32)
        return jax.nn.sigmoid(g_ref[0].astype(F32) + bg_ref[:, k * d:(k + 1) * d]) * br

    m = (branch(ya_ref, wa_ref, ga_ref, 0) + branch(zb_ref, wb_ref, gb_ref, 1)
         + branch(oc_ref, wc_ref, gc_ref, 2))
    out = jnp.dot(m.astype(BF16), wo_ref[...], preferred_element_type=F32)
    o_ref[0] = x_ref[0] + mg_ref[0] * out


def gated_merge(ya, zb, oc, y, b_gate, w_a_out, w_b_out, w_c_out, w_o, x, mod_gate):
    b, t, d = x.shape
    tm = min(256, t)
    per_batch = mod_gate.shape[0] > 1
    row = lambda width, cb: pl.BlockSpec((1, tm, width), lambda bi, i: (bi, i, cb))
    const = lambda shape: pl.BlockSpec(shape, lambda bi, i: (0,) * len(shape),
                                       pipeline_mode=pl.Buffered(1))
    g0 = COL_GATE // d
    return pl.pallas_call(
        _merge_kernel,
        out_shape=jax.ShapeDtypeStruct((b, t, d), F32),
        grid=(b, t // tm),
        in_specs=[row(W_CONV, 0), row(W_HYENA, 0), row(W_DIFF, 0),
                  row(d, g0), row(d, g0 + 1), row(d, g0 + 2),
                  const((1, 3 * d)), const((W_CONV, d)), const((W_HYENA, d)), const((W_DIFF, d)),
                  const((d, d)), row(d, 0),
                  pl.BlockSpec((1, 1, d), (lambda bi, i: (bi, 0, 0)) if per_batch
                               else (lambda bi, i: (0, 0, 0)))],
        out_specs=row(d, 0),
        compiler_params=_params(2),
        name="gated_merge",
    )(ya, zb, oc, y, y, y, b_gate.reshape(1, 3 * d), w_a_out.astype(BF16), w_b_out.astype(BF16),
      w_c_out.astype(BF16), w_o.astype(BF16), x, mod_gate)


def _ffn_down_kernel(*refs, tm, final):
    (a_ref, v_ref, ap_ref, vp_ref, an_ref, vn_ref, cwa_ref, cwv_ref, cba_ref, cbv_ref, w_ref, x_ref,
     mg_ref) = refs[:13]
    o_ref = refs[-1]
    i = pl.program_id(1)
    k = pl.program_id(2)
    last_i = pl.num_programs(1) - 1
    last_k = pl.num_programs(2) - 1

    def conv(c_ref, p_ref, n_ref, w_ref_, b_ref):
        c = c_ref[0].astype(F32)
        rows = lax.broadcasted_iota(jnp.int32, c.shape, 0)
        prev_row = jnp.where(i == 0, 0.0, p_ref[0, HALO - 1:HALO, :].astype(F32))
        next_row = jnp.where(i == last_i, 0.0, n_ref[0, 0:1, :].astype(F32))
        up = jnp.where(rows == 0, prev_row, pltpu.roll(c, 1, 0))
        dn = jnp.where(rows == tm - 1, next_row, pltpu.roll(c, tm - 1, 0))
        return w_ref_[0:1, :] * up + w_ref_[1:2, :] * c + w_ref_[2:3, :] * dn + b_ref[...]

    a = conv(a_ref, ap_ref, an_ref, cwa_ref, cba_ref)
    v = conv(v_ref, vp_ref, vn_ref, cwv_ref, cbv_ref)
    hidden = (a * jax.nn.sigmoid(a) * v).astype(BF16)
    part = jnp.dot(hidden, w_ref[...].astype(BF16), preferred_element_type=F32)

    @pl.when(k == 0)
    def _():
        o_ref[0] = part

    @pl.when(k > 0)
    def _():
        o_ref[0] = o_ref[0] + part

    @pl.when(k == last_k)
    def _():
        out = x_ref[0] + mg_ref[0] * o_ref[0]
        if final:
            out = _rms(out, refs[13][...])
        o_ref[0] = out


def ffn_down(u, conv_w, conv_b, w_down, x, mod_gate, final_g=None):
    b, t, d = x.shape
    tk = 512
    nk = D_FF // tk
    tm = min(512, t)
    hb = tm // HALO
    n_hb = t // HALO
    per_batch = mod_gate.shape[0] > 1
    main = lambda off: pl.BlockSpec((1, tm, tk), lambda bi, i, k: (bi, i, off + k))
    prev = lambda off: pl.BlockSpec((1, HALO, tk),
                                    lambda bi, i, k: (bi, jnp.maximum(i * hb - 1, 0), off + k))
    nxt = lambda off: pl.BlockSpec((1, HALO, tk),
                                   lambda bi, i, k: (bi, jnp.minimum((i + 1) * hb, n_hb - 1), off + k))
    vec = lambda rows, off: pl.BlockSpec((rows, tk), lambda bi, i, k: (0, off + k))
    in_specs = [main(0), main(nk), prev(0), prev(nk), nxt(0), nxt(nk),
                vec(3, 0), vec(3, nk), vec(1, 0), vec(1, nk),
                pl.BlockSpec((tk, d), lambda bi, i, k: (k, 0)),
                pl.BlockSpec((1, tm, d), lambda bi, i, k: (bi, i, 0)),
                pl.BlockSpec((1, 1, d), (lambda bi, i, k: (bi, 0, 0)) if per_batch
                             else (lambda bi, i, k: (0, 0, 0)))]
    cb = conv_b.reshape(1, 2 * D_FF)
    args = [u, u, u, u, u, u, conv_w, conv_w, cb, cb, w_down, x, mod_gate]
    if final_g is not None:
        in_specs.append(pl.BlockSpec((1, d), lambda bi, i, k: (0, 0)))
        args.append(final_g.reshape(1, d))
    kern = functools.partial(_ffn_down_kernel, tm=tm, final=final_g is not None)
    return pl.pallas_call(
        kern,
        out_shape=jax.ShapeDtypeStruct((b, t, d), F32),
        grid=(b, t // tm, nk),
        in_specs=in_specs,
        out_specs=pl.BlockSpec((1, tm, d), lambda bi, i, k: (bi, i, 0)),
        compiler_params=_params(3),
        name="ffn_down",
    )(*args)


def mixer_sublayer(x, y, y_c, y_lat, rope, p, mod_gate, lam_init):
    ya = depthwise_conv(y, COL_A, W_CONV, p['conv_a_w'], p['conv_a_b'], glu=True,
                        ln=(p['ln_a_g'], p['ln_a_b']), out_dtype=BF16, name="conformer_conv")
    u = depthwise_conv(y, COL_B, 3 * W_HYENA, p['short_b_w'], p['short_b_b'], name="hyena_short_conv")
    filt = (p['filt_w1'], p['filt_b1'], p['filt_w2'], p['filt_b2'], p['filt_w3'])
    zb = hyena_long_conv(u, p['hyena_skip'], filt)
    oc = diff_attention(y, y_c, y_lat, rope, p['diff_lambda'], p['subln_g'], lam_init)
    return gated_merge(ya, zb, oc, y, p['b_gate'], p['w_a_out'], p['w_b_out'], p['w_c_out'],
                       p['w_o'], x, mod_gate)


def kernel(x, c, ctx, c_ctx, w_ada, b_ada, norm1_g, norm2_g, w_in, b_gate, conv_a_w, conv_a_b, ln_a_g, ln_a_b, w_a_out, short_b_w, short_b_b, filt_w1, filt_b1, filt_w2, filt_b2, filt_w3, hyena_skip, w_b_out, diff_lambda, subln_g, w_c_out, w_o, w_up, conv_f_w, conv_f_b, w_down, final_g):
    batch = x.shape[0]
    rope = rope_tables(x.shape[1])
    cc = jnp.concatenate([c, c_ctx[None, :], jnp.zeros((8 - batch - 1, D_MODEL), F32)], axis=0)
    gate_rot = N_MIX // TN
    for l in range(DEPTH):
        p = {
            'b_gate': b_gate[l],
            'conv_a_w': conv_a_w[l], 'conv_a_b': conv_a_b[l],
            'ln_a_g': ln_a_g[l], 'ln_a_b': ln_a_b[l], 'w_a_out': w_a_out[l],
            'short_b_w': short_b_w[l], 'short_b_b': short_b_b[l],
            'filt_w1': filt_w1[l], 'filt_b1': filt_b1[l], 'filt_w2': filt_w2[l],
            'filt_b2': filt_b2[l], 'filt_w3': filt_w3[l],
            'hyena_skip': hyena_skip[l], 'w_b_out': w_b_out[l],
            'diff_lambda': diff_lambda[l], 'subln_g': subln_g[l], 'w_c_out': w_c_out[l],
            'w_o': w_o[l],
        }
        update_ctx = l < DEPTH - 1
        last = l == DEPTH - 1
        lam_init = 0.8 - 0.6 * math.exp(-0.3 * l)
        mod = ada_projection(cc, w_ada[l], b_ada[l])
        mx = [mod[:batch, None, j * D_MODEL:(j + 1) * D_MODEL] for j in range(6)]
        mc = [mod[batch:batch + 1, None, j * D_MODEL:(j + 1) * D_MODEL] for j in range(6)]

        y_x = norm_mod_matmul(x, norm1_g[l], mx[0], mx[1], w_in[l], gate_rot, "mm_in_x")
        y_c = norm_mod_matmul(ctx, norm1_g[l], mc[0], mc[1], w_in[l], gate_rot, "mm_in_c")
        x = mixer_sublayer(x, y_x, y_c, y_x, rope, p, mx[2], lam_init)
        u = norm_mod_matmul(x, norm2_g[l], mx[3], mx[4], w_up[l], 0, "mm_up_x")
        x = ffn_down(u, conv_f_w[l], conv_f_b[l], w_down[l], x, mx[5], final_g if last else None)
        if update_ctx:
            ctx = mixer_sublayer(ctx, y_c, y_c, None, None, p, mc[2], lam_init)
            u = norm_mod_matmul(ctx, norm2_g[l], mc[3], mc[4], w_up[l], 0, "mm_up_c")
            ctx = ffn_down(u, conv_f_w[l], conv_f_b[l], w_down[l], ctx, mc[5])
    return x
```

```python
import functools
import math

import numpy as np
import jax
import jax.numpy as jnp
from jax import lax
from jax.experimental import pallas as pl
from jax.experimental.pallas import tpu as pltpu

D_MODEL = 2048
DEPTH = 2
GRID_W = 64
W_CONV = D_MODEL // 4
W_HYENA = D_MODEL // 4
N_HEADS = 8
HEAD_DIM = 64
HEAD_W = 2 * HEAD_DIM
W_DIFF = N_HEADS * HEAD_W
D_FF = 5632
HYENA_ORDER = 2
HYENA_BANDS = 8
HYENA_TARGET = 1e-2
HYENA_FAST_PCT = 0.3
HYENA_SLOW_PCT = 1.5
ROPE_BASE = 10000.0
EPS = 1e-6
LOG2E = 1.4426950408889634

LANE = 128
HALO = 16
FREQ_BLOCK = 256
TN = 512
VMEM_LIMIT_BYTES = 56 * 1024 * 1024

N_IN = 2 * W_CONV + 3 * W_HYENA + 3 * W_DIFF + 3 * D_MODEL
N_MIX = N_IN - 3 * D_MODEL
COL_GATE = 0
COL_A = 3 * D_MODEL
COL_B = COL_A + 2 * W_CONV
COL_Q = COL_B + 3 * W_HYENA
COL_K = COL_Q + W_DIFF
COL_V = COL_K + W_DIFF

F32 = jnp.float32
BF16 = jnp.bfloat16


def _params(n_axes):
    return pltpu.CompilerParams(
        dimension_semantics=("arbitrary",) * n_axes, vmem_limit_bytes=VMEM_LIMIT_BYTES)


def _rms(x, g):
    return x * lax.rsqrt(jnp.mean(x * x, axis=-1, keepdims=True) + EPS) * g


def _ada_kernel(c_ref, w_ref, b_ref, o_ref):
    c = c_ref[...]
    h = c * jax.nn.sigmoid(c)
    o_ref[...] = jnp.dot(h, w_ref[...], precision=lax.Precision.HIGHEST,
                         preferred_element_type=F32) + b_ref[...]


def ada_projection(cc, w, layer, b):
    m, d = cc.shape
    n = w.shape[2]
    tn = 1024
    return pl.pallas_call(
        _ada_kernel,
        out_shape=jax.ShapeDtypeStruct((m, n), F32),
        grid=(n // tn,),
        in_specs=[pl.BlockSpec((m, d), lambda j: (0, 0)),
                  pl.BlockSpec((None, d, tn), lambda j: (layer, 0, j)),
                  pl.BlockSpec((1, tn), lambda j: (0, j))],
        out_specs=pl.BlockSpec((m, tn), lambda j: (0, j)),
        compiler_params=_params(1),
        name="ada_proj",
    )(cc, w, b.reshape(1, n))


def _nmm_kernel(x_ref, g_ref, sh_ref, sc_ref, w_ref, o_ref, h_ref):
    @pl.when(pl.program_id(2) == 0)
    def _():
        rb = min(256, h_ref.shape[0])

        def rows(r, carry):
            r0 = pl.multiple_of(r * rb, rb)
            y = _rms(x_ref[0, pl.ds(r0, rb), :], g_ref[...])
            h_ref[pl.ds(r0, rb), :] = (y * (1.0 + sc_ref[0]) + sh_ref[0]).astype(BF16)
            return carry

        lax.fori_loop(0, h_ref.shape[0] // rb, rows, 0)

    o_ref[0] = jnp.dot(h_ref[...], w_ref[...], preferred_element_type=F32).astype(o_ref.dtype)


def norm_mod_matmul(x, g, shift, scale, w, layer, col_rot, name):
    b, t, d = x.shape
    n = w.shape[2]
    tm = min(2048, t)
    nt = n // TN
    per_batch = shift.shape[0] > 1
    mod_spec = pl.BlockSpec((1, 1, d), (lambda bi, i, j: (bi, 0, 0)) if per_batch
                            else (lambda bi, i, j: (0, 0, 0)))
    return pl.pallas_call(
        _nmm_kernel,
        out_shape=jax.ShapeDtypeStruct((b, t, n), BF16),
        grid=(b, t // tm, nt),
        in_specs=[pl.BlockSpec((1, tm, d), lambda bi, i, j: (bi, i, 0)),
                  pl.BlockSpec((1, d), lambda bi, i, j: (0, 0)),
                  mod_spec, mod_spec,
                  pl.BlockSpec((None, d, TN), lambda bi, i, j: (layer, 0, (j + col_rot) % nt))],
        out_specs=pl.BlockSpec((1, tm, TN), lambda bi, i, j: (bi, i, j)),
        scratch_shapes=[pltpu.VMEM((tm, d), BF16)],
        compiler_params=_params(3),
        name=name,
    )(x, g.reshape(1, d), shift, scale, w)


def _dwconv_kernel(*refs, taps, glu, post, tt, rb):
    n_in = 2 if glu else 1
    main = refs[:n_in]
    prev = refs[n_in:2 * n_in]
    nxt = refs[2 * n_in:3 * n_in]
    rest = refs[3 * n_in:]
    if post:
        w_ref, b_ref, lg_ref, lb_ref, o_ref, ext_ref = rest
    else:
        w_ref, b_ref, o_ref, ext_ref = rest
    i = pl.program_id(1)
    last = pl.num_programs(1) - 1

    def value(blocks):
        a = blocks[0][0].astype(F32)
        if glu:
            a = a * jax.nn.sigmoid(blocks[1][0].astype(F32))
        return a

    ext_ref[0:HALO, :] = jnp.where(i == 0, 0.0, value(prev))
    ext_ref[HALO:HALO + tt, :] = value(main)
    ext_ref[HALO + tt:2 * HALO + tt, :] = jnp.where(i == last, 0.0, value(nxt))

    pad = (taps - 1) // 2
    for r in range(tt // rb):
        acc = jnp.zeros((rb, ext_ref.shape[1]), F32) + b_ref[...]
        for j in range(taps):
            acc = acc + w_ref[j:j + 1, :] * ext_ref[pl.ds(HALO - pad + j + r * rb, rb), :]
        if post:
            mu = jnp.mean(acc, axis=-1, keepdims=True)
            cen = acc - mu
            var = jnp.mean(cen * cen, axis=-1, keepdims=True)
            y = cen * lax.rsqrt(var + EPS) * lg_ref[...] + lb_ref[...]
            acc = y * jax.nn.sigmoid(y)
        o_ref[0, r * rb:(r + 1) * rb, :] = acc.astype(o_ref.dtype)


def depthwise_conv(y, col, width, w, bias, glu=False, ln=None, out_dtype=F32, name="dwconv"):
    b, t, _ = y.shape
    taps = w.shape[0]
    tt = min(512, t)
    cw = width if ln is not None else min(512, width)
    ncb = width // cw
    cb0 = col // cw
    hb = tt // HALO
    n_hb = t // HALO
    cols = [cb0, cb0 + ncb] if glu else [cb0]

    def main_spec(c0):
        return pl.BlockSpec((1, tt, cw), lambda bi, i, c: (bi, i, c0 + c))

    def prev_spec(c0):
        return pl.BlockSpec((1, HALO, cw), lambda bi, i, c: (bi, jnp.maximum(i * hb - 1, 0), c0 + c))

    def next_spec(c0):
        return pl.BlockSpec((1, HALO, cw),
                            lambda bi, i, c: (bi, jnp.minimum((i + 1) * hb, n_hb - 1), c0 + c))

    vec = lambda rows: pl.BlockSpec((rows, cw), lambda bi, i, c: (0, c))
    in_specs = ([main_spec(c0) for c0 in cols] + [prev_spec(c0) for c0 in cols]
                + [next_spec(c0) for c0 in cols] + [vec(taps), vec(1)])
    args = [y] * (3 * len(cols)) + [w, bias.reshape(1, width)]
    if ln is not None:
        in_specs += [vec(1), vec(1)]
        args += [ln[0].reshape(1, width), ln[1].reshape(1, width)]
    kern = functools.partial(_dwconv_kernel, taps=taps, glu=glu, post=ln is not None, tt=tt,
                             rb=min(128, tt))
    return pl.pallas_call(
        kern,
        out_shape=jax.ShapeDtypeStruct((b, t, width), out_dtype),
        grid=(b, t // tt, ncb),
        in_specs=in_specs,
        out_specs=pl.BlockSpec((1, tt, cw), lambda bi, i, c: (bi, i, c)),
        scratch_shapes=[pltpu.VMEM((tt + 2 * HALO, cw), F32)],
        compiler_params=_params(3),
        name=name,
    )(*args)


def _swap_halves(x):
    lanes = lax.broadcasted_iota(jnp.int32, x.shape, 1)
    first = (lanes % HEAD_DIM) < (HEAD_DIM // 2)
    return jnp.where(first, pltpu.roll(x, LANE - HEAD_DIM // 2, 1), pltpu.roll(x, HEAD_DIM // 2, 1))


def _attn_kernel(*refs, lam_init, n_sub, t_c, has_lat):
    if has_lat:
        (q_ref, kc_ref, vc_ref, kx_ref, vx_ref, cq_ref, sq_ref, ck_ref, sk_ref, dl_ref, g_ref,
         o_ref, s_ref, p_ref, k_s, vt_s) = refs
    else:
        q_ref, kc_ref, vc_ref, dl_ref, g_ref, o_ref, s_ref, p_ref, k_s, vt_s = refs
    t_k = k_s.shape[0]
    tq = q_ref.shape[1]
    ts = tq // n_sub
    ck = 256

    @pl.when(pl.program_id(2) == 0)
    def _():
        k_s[0:t_c, :] = kc_ref[0]
        if has_lat:
            kx = kx_ref[0].astype(F32)
            k_s[t_c:, :] = (kx * ck_ref[...] + _swap_halves(kx) * sk_ref[...]).astype(BF16)
        for c in range(t_k // ck):
            lo = c * ck
            src = vc_ref[0, lo:lo + ck, :] if lo < t_c else vx_ref[0, lo - t_c:lo - t_c + ck, :]
            vt_s[:, lo:lo + ck] = src.astype(F32).T.astype(BF16)

    q = q_ref[0].astype(F32)
    if has_lat:
        q = q * cq_ref[...] + _swap_halves(q) * sq_ref[...]
    qt = (q * (HEAD_DIM ** -0.5 * LOG2E)).T
    rows = lax.broadcasted_iota(jnp.int32, qt.shape, 0)
    qts = (jnp.where(rows < HEAD_DIM, qt, 0.0).astype(BF16),
           jnp.where(rows >= HEAD_DIM, qt, 0.0).astype(BF16))

    kk = k_s[...]
    vt = vt_s[...]
    streams = [(mi, si) for si in range(n_sub) for mi in range(2)]
    ms = {}
    for mi, si in streams:
        s = jnp.dot(kk, qts[mi][:, si * ts:(si + 1) * ts], preferred_element_type=F32)
        s_ref[mi, si] = s
        ms[mi, si] = jnp.max(s, axis=0, keepdims=True)
    outs = {}
    for mi, si in streams:
        p = jnp.exp2(s_ref[mi, si] - ms[mi, si])
        l = jnp.sum(p, axis=0, keepdims=True)
        p_ref[mi, si] = p.astype(BF16)
        acc = jnp.dot(vt, p_ref[mi, si], preferred_element_type=F32)
        outs[mi, si] = acc / l

    dl = dl_ref[...]
    s1 = jnp.sum(dl[0:1] * dl[1:2], axis=-1, keepdims=True)
    s2 = jnp.sum(dl[2:3] * dl[3:4], axis=-1, keepdims=True)
    lam = jnp.exp(s1) - jnp.exp(s2) + lam_init
    for si in range(n_sub):
        ot = outs[0, si] - lam * outs[1, si]
        ot = ot * lax.rsqrt(jnp.mean(ot * ot, axis=0, keepdims=True) + EPS)
        o = ot.T
        o_ref[0, si * ts:(si + 1) * ts, :] = (o * g_ref[0] * (1.0 - lam_init)).astype(o_ref.dtype)


def diff_attention(y_q, y_c, y_x, rope, diff_lambda, subln_g, lam_init):
    b, t_q, _ = y_q.shape
    t_c = y_c.shape[1]
    has_lat = y_x is not None
    t_k = t_c + (y_x.shape[1] if has_lat else 0)
    tq = min(512, t_q)
    ts = min(256, tq)
    n_sub = tq // ts
    qb, kb, vb = COL_Q // HEAD_W, COL_K // HEAD_W, COL_V // HEAD_W
    head = lambda base, rows: pl.BlockSpec((1, rows, HEAD_W), lambda bi, hi, i: (bi, 0, base + hi))
    in_specs = [pl.BlockSpec((1, tq, HEAD_W), lambda bi, hi, i: (bi, i, qb + hi)),
                head(kb, t_c), head(vb, t_c)]
    args = [y_q, y_c, y_c]
    if has_lat:
        t_x = y_x.shape[1]
        in_specs += [head(kb, t_x), head(vb, t_x),
                     pl.BlockSpec((tq, HEAD_W), lambda bi, hi, i: (i, 0)),
                     pl.BlockSpec((tq, HEAD_W), lambda bi, hi, i: (i, 0)),
                     pl.BlockSpec((t_x, HEAD_W), lambda bi, hi, i: (0, 0)),
                     pl.BlockSpec((t_x, HEAD_W), lambda bi, hi, i: (0, 0))]
        args += [y_x, y_x, rope[0], rope[1], rope[0], rope[1]]
    in_specs += [pl.BlockSpec((4, HEAD_DIM), lambda bi, hi, i: (0, 0)),
                 pl.BlockSpec((1, 1, HEAD_W), lambda bi, hi, i: (hi, 0, 0))]
    args += [diff_lambda.astype(F32), subln_g.astype(F32).reshape(N_HEADS, 1, HEAD_W)]
    kern = functools.partial(_attn_kernel, lam_init=lam_init, n_sub=n_sub, t_c=t_c, has_lat=has_lat)
    return pl.pallas_call(
        kern,
        out_shape=jax.ShapeDtypeStruct((b, t_q, W_DIFF), BF16),
        grid=(b, N_HEADS, t_q // tq),
        in_specs=in_specs,
        out_specs=pl.BlockSpec((1, tq, HEAD_W), lambda bi, hi, i: (bi, i, hi)),
        scratch_shapes=[pltpu.VMEM((2, n_sub, t_k, ts), F32),
                        pltpu.VMEM((2, n_sub, t_k, ts), BF16),
                        pltpu.VMEM((t_k, HEAD_W), BF16),
                        pltpu.VMEM((HEAD_W, t_k), BF16)],
        compiler_params=_params(3),
        name="diff_attn",
    )(*args)


def rope_tables(t):
    rows = t // GRID_W
    row = np.repeat(np.arange(rows), GRID_W).astype(np.float64)
    col = np.tile(np.arange(GRID_W), rows).astype(np.float64)
    n_freq = HEAD_DIM // 4
    inv = (ROPE_BASE ** (-np.arange(n_freq, dtype=np.float32) / n_freq)).astype(np.float64)
    ang = np.concatenate([row[:, None] * inv, col[:, None] * inv], axis=-1)
    c, s = np.cos(ang), np.sin(ang)
    cos_t = np.tile(np.concatenate([c, c], axis=-1), (1, 2))
    sin_t = np.tile(np.concatenate([-s, s], axis=-1), (1, 2))
    return jnp.asarray(cos_t, F32), jnp.asarray(sin_t, F32)


@functools.lru_cache(maxsize=None)
def _dft_factors(length):
    n = 2 * length
    fb = min(FREQ_BLOCK, length)
    nb = length // fb
    kk = np.arange(length, dtype=np.int64).reshape(nb, 1, fb)
    k_rows = np.concatenate([kk, kk], axis=1)
    is_sin = np.zeros((nb, 2, fb), bool)
    is_sin[:, 1, :] = True
    k_rows[0, 1, 0] = length
    is_sin[0, 1, 0] = False
    k_rows = k_rows.reshape(-1)
    is_sin = is_sin.reshape(-1)
    n_hi = max(length // LANE, 1)
    n_lo = min(LANE, length)
    a_lo = 2.0 * np.pi * ((k_rows[:, None] * np.arange(n_lo)[None, :]) % n) / n
    a_hi = 2.0 * np.pi * ((k_rows[:, None] * (LANE * np.arange(n_hi))[None, :]) % n) / n
    c_lo, s_lo, c_hi, s_hi = np.cos(a_lo), np.sin(a_lo), np.cos(a_hi), np.sin(a_hi)
    p = np.where(is_sin[:, None], s_hi, c_hi)
    q = np.where(is_sin[:, None], c_hi, -s_hi)
    scale = np.full((2 * length,), 2.0 / n)
    scale[0] = 1.0 / n
    scale[fb] = 1.0 / n
    f32 = lambda a: np.asarray(a, np.float32)
    return f32(p), f32(q), f32(c_lo), f32(s_lo), f32(scale.reshape(-1, 1))


def dft_tables(length):
    p, q, c_lo, s_lo, _ = _dft_factors(length)
    fwd = (jnp.asarray(p)[:, :, None] * jnp.asarray(c_lo)[:, None, :]
           + jnp.asarray(q)[:, :, None] * jnp.asarray(s_lo)[:, None, :])
    fwd = fwd.reshape(2 * length, length).astype(BF16)
    return fwd, fwd.T


def _cmul(za, zb, ha, hb, first_rows):
    ya = za * ha - zb * hb
    yb = za * hb + zb * ha
    ya = jnp.where(first_rows, za * ha, ya)
    yb = jnp.where(first_rows, zb * hb, yb)
    return ya, yb


def _fwd_kernel(f_ref, z_ref, h_ref, c_ref, y_ref, zb_ref, *, fb):
    @pl.when(pl.program_id(1) == 0)
    def _():
        zb_ref[...] = z_ref[0].astype(BF16)

    spec = jnp.dot(f_ref[...], zb_ref[...], preferred_element_type=F32)
    za, zb = spec[:fb], spec[fb:]
    h = h_ref[...]
    rows = lax.broadcasted_iota(jnp.int32, za.shape, 0)
    first = jnp.logical_and(rows == 0, pl.program_id(1) == 0)
    ya, yb = _cmul(za, zb, h[:fb], h[fb:], first)
    c = c_ref[...]
    y_ref[0, :fb, :] = (ya * c[:fb]).astype(y_ref.dtype)
    y_ref[0, fb:, :] = (yb * c[fb:]).astype(y_ref.dtype)


def hyena_forward(z, z_col, hspec, order, fwd, length):
    b = z.shape[0]
    w = W_HYENA
    scale = jnp.asarray(_dft_factors(length)[4])
    fb = min(FREQ_BLOCK, length)
    kern = functools.partial(_fwd_kernel, fb=fb)
    return pl.pallas_call(
        kern,
        out_shape=jax.ShapeDtypeStruct((b, 2 * length, w), BF16),
        grid=(b, length // fb),
        in_specs=[pl.BlockSpec((2 * fb, length), lambda bi, i: (i, 0)),
                  pl.BlockSpec((1, length, w), lambda bi, i: (bi, 0, z_col)),
                  pl.BlockSpec((2 * fb, w), lambda bi, i: (i, order)),
                  pl.BlockSpec((2 * fb, 1), lambda bi, i: (i, 0))],
        out_specs=pl.BlockSpec((1, 2 * fb, w), lambda bi, i: (bi, i, 0)),
        scratch_shapes=[pltpu.VMEM((length, w), BF16)],
        compiler_params=_params(2),
        name="hyena_fwd",
    )(fwd, z, hspec, scale)


def _inv_kernel(g_ref, y_ref, z_ref, gate_ref, skip_ref, o_ref):
    conv = jnp.dot(g_ref[...], y_ref[0], preferred_element_type=F32)
    o_ref[0] = (gate_ref[0] * (conv + z_ref[0] * skip_ref[...])).astype(o_ref.dtype)


def hyena_inverse(y, z, z_col, gate, gate_col, skip, inv, length, out_dtype):
    b = z.shape[0]
    w = W_HYENA
    tm = min(512, length)
    return pl.pallas_call(
        _inv_kernel,
        out_shape=jax.ShapeDtypeStruct((b, length, w), out_dtype),
        grid=(b, length // tm),
        in_specs=[pl.BlockSpec((tm, 2 * length), lambda bi, i: (i, 0)),
                  pl.BlockSpec((1, 2 * length, w), lambda bi, i: (bi, 0, 0)),
                  pl.BlockSpec((1, tm, w), lambda bi, i: (bi, i, z_col)),
                  pl.BlockSpec((1, tm, w), lambda bi, i: (bi, i, gate_col)),
                  pl.BlockSpec((1, w), lambda bi, i: (0, 0))],
        out_specs=pl.BlockSpec((1, tm, w), lambda bi, i: (bi, i, 0)),
        compiler_params=_params(2),
        name="hyena_inv",
    )(inv, y, z, gate, skip.reshape(1, w))


def _filt_kernel(feat_ref, w1_ref, b1_ref, w2_ref, b2_ref, w3f_ref, w3b_ref, dl_ref,
                 sum_ref, diff_ref, hf_ref, hb_ref, *, rb):
    hi = lax.Precision.HIGHEST
    length, ct = hf_ref.shape
    n_blocks = length // rb

    def taps(i, norm):
        r0 = pl.multiple_of(i * rb, rb)
        feats = feat_ref[pl.ds(r0, rb), :]
        h = jnp.sin(jnp.dot(feats, w1_ref[...], precision=hi, preferred_element_type=F32)
                    + b1_ref[...])
        h = jnp.sin(jnp.dot(h, w2_ref[...], precision=hi, preferred_element_type=F32)
                    + b2_ref[...])
        decay = jnp.exp(-feats[:, 0:1] * dl_ref[...])
        hf = jnp.dot(h, w3f_ref[...], precision=hi, preferred_element_type=F32) * decay
        hb = jnp.dot(h, w3b_ref[...], precision=hi, preferred_element_type=F32) * decay
        rows = lax.broadcasted_iota(jnp.int32, hb.shape, 0) + r0
        hb = jnp.where(rows == 0, 0.0, hb)
        hf_ref[pl.ds(r0, rb), :] = hf
        hb_ref[pl.ds(r0, rb), :] = hb
        return (norm + jnp.sum(jnp.abs(hf), axis=0, keepdims=True)
                + jnp.sum(jnp.abs(hb), axis=0, keepdims=True))

    norm = lax.fori_loop(0, n_blocks, taps, jnp.zeros((1, ct), F32))
    inv = 1.0 / norm

    def normalise(i, carry):
        r0 = pl.multiple_of(i * rb, rb)
        hf = hf_ref[pl.ds(r0, rb), :]
        hb = hb_ref[pl.ds(r0, rb), :]
        sum_ref[pl.ds(r0, rb), :] = ((hf + hb) * inv).astype(sum_ref.dtype)
        diff_ref[pl.ds(r0, rb), :] = ((hf - hb) * inv).astype(diff_ref.dtype)
        return carry

    lax.fori_loop(0, n_blocks, normalise, 0)


def hyena_filter_taps(length, filt_w1, filt_b1, filt_w2, filt_b2, filt_w3):
    w = W_HYENA
    ct = 256
    pos = np.arange(length, dtype=np.float64)
    t = pos / max(length - 1, 1)
    bands = np.linspace(1e-4, HYENA_BANDS - 1, HYENA_BANDS)
    ang = (2.0 * math.pi * pos / length)[:, None] * bands[None, :]
    feats = np.concatenate([t[:, None], np.cos(ang), -np.sin(ang)], axis=-1).astype(np.float32)
    deltas = np.abs(np.linspace(math.log(HYENA_TARGET) / HYENA_SLOW_PCT,
                                math.log(HYENA_TARGET) / HYENA_FAST_PCT, w)).astype(np.float32)
    emb, nf = filt_w1.shape
    npc = w // ct
    full = lambda shape: pl.BlockSpec(shape, lambda o, c: (0,) * len(shape))
    kern = functools.partial(_filt_kernel, rb=min(256, length))
    return pl.pallas_call(
        kern,
        out_shape=(jax.ShapeDtypeStruct((length, HYENA_ORDER * w), BF16),
                   jax.ShapeDtypeStruct((length, HYENA_ORDER * w), BF16)),
        grid=(HYENA_ORDER, npc),
        in_specs=[full((length, emb)), full((emb, nf)), full((1, nf)), full((nf, nf)), full((1, nf)),
                  pl.BlockSpec((nf, ct), lambda o, c: (0, o * 2 * npc + c)),
                  pl.BlockSpec((nf, ct), lambda o, c: (0, o * 2 * npc + npc + c)),
                  pl.BlockSpec((1, ct), lambda o, c: (0, c))],
        out_specs=(pl.BlockSpec((length, ct), lambda o, c: (0, o * npc + c)),
                   pl.BlockSpec((length, ct), lambda o, c: (0, o * npc + c))),
        scratch_shapes=[pltpu.VMEM((length, ct), F32),
                        pltpu.VMEM((length, ct), F32)],
        compiler_params=_params(2),
        name="hyena_filt",
    )(jnp.asarray(feats), filt_w1, filt_b1.reshape(1, nf), filt_w2, filt_b2.reshape(1, nf),
      filt_w3, filt_w3, jnp.asarray(deltas).reshape(1, w))


def _hspec_kernel(f_ref, hs_ref, hd_ref, o_ref, *, fb):
    f = f_ref[...]
    a = jnp.dot(f, hs_ref[...], preferred_element_type=F32)
    bd = jnp.dot(f[fb:], hd_ref[...], preferred_element_type=F32)
    rows = lax.broadcasted_iota(jnp.int32, bd.shape, 0)
    first = jnp.logical_and(rows == 0, pl.program_id(1) == 0)
    o_ref[:fb, :] = a[:fb]
    o_ref[fb:, :] = jnp.where(first, a[fb:], bd)


def hyena_filter_spectrum(hsum, hdiff, fwd, length):
    fb = min(FREQ_BLOCK, length)
    cols = hsum.shape[1]
    w = W_HYENA
    kern = functools.partial(_hspec_kernel, fb=fb)
    return pl.pallas_call(
        kern,
        out_shape=jax.ShapeDtypeStruct((2 * length, cols), F32),
        grid=(cols // w, length // fb),
        in_specs=[pl.BlockSpec((2 * fb, length), lambda j, i: (i, 0)),
                  pl.BlockSpec((length, w), lambda j, i: (0, j)),
                  pl.BlockSpec((length, w), lambda j, i: (0, j))],
        out_specs=pl.BlockSpec((2 * fb, w), lambda j, i: (i, j)),
        compiler_params=_params(2),
        name="hyena_hspec",
    )(fwd, hsum, hdiff)


def hyena_long_conv(u, hyena_skip, filt):
    length = u.shape[1]
    fwd, inv = dft_tables(length)
    hsum, hdiff = hyena_filter_taps(length, *filt)
    hspec = hyena_filter_spectrum(hsum, hdiff, fwd, length)
    y = hyena_forward(u, 0, hspec, 0, fwd, length)
    z = hyena_inverse(y, u, 0, u, 1, hyena_skip[0], inv, length, F32)
    y = hyena_forward(z, 0, hspec, 1, fwd, length)
    return hyena_inverse(y, z, 0, u, 2, hyena_skip[1], inv, length, BF16)


def _merge_kernel(ya_ref, zb_ref, oc_ref, ga_ref, gb_ref, gc_ref, bg_ref, wa_ref, wb_ref, wc_ref,
                  wo_ref, x_ref, mg_ref, o_ref):
    d = D_MODEL

    def branch(act_ref, w_ref, g_ref, k):
        br = jnp.dot(act_ref[0], w_ref[...], preferred_element_type=F32)
        return jax.nn.sigmoid(g_ref[0].astype(F32) + bg_ref[:, k * d:(k + 1) * d]) * br

    m = (branch(ya_ref, wa_ref, ga_ref, 0) + branch(zb_ref, wb_ref, gb_ref, 1)
         + branch(oc_ref, wc_ref, gc_ref, 2))
    out = jnp.dot(m.astype(BF16), wo_ref[...], preferred_element_type=F32)
    o_ref[0] = x_ref[0] + mg_ref[0] * out


def gated_merge(ya, zb, oc, y, b_gate, w_a_out, w_b_out, w_c_out, w_o, x, mod_gate):
    b, t, d = x.shape
    tm = min(256, t)
    per_batch = mod_gate.shape[0] > 1
    row = lambda width, cb: pl.BlockSpec((1, tm, width), lambda bi, i: (bi, i, cb))
    const = lambda shape: pl.BlockSpec(shape, lambda bi, i: (0,) * len(shape),
                                       pipeline_mode=pl.Buffered(1))
    g0 = COL_GATE // d
    return pl.pallas_call(
        _merge_kernel,
        out_shape=jax.ShapeDtypeStruct((b, t, d), F32),
        grid=(b, t // tm),
        in_specs=[row(W_CONV, 0), row(W_HYENA, 0), row(W_DIFF, 0),
                  row(d, g0), row(d, g0 + 1), row(d, g0 + 2),
                  const((1, 3 * d)), const((W_CONV, d)), const((W_HYENA, d)), const((W_DIFF, d)),
                  const((d, d)), row(d, 0),
                  pl.BlockSpec((1, 1, d), (lambda bi, i: (bi, 0, 0)) if per_batch
                               else (lambda bi, i: (0, 0, 0)))],
        out_specs=row(d, 0),
        compiler_params=_params(2),
        name="gated_merge",
    )(ya, zb, oc, y, y, y, b_gate.reshape(1, 3 * d), w_a_out.astype(BF16), w_b_out.astype(BF16),
      w_c_out.astype(BF16), w_o.astype(BF16), x, mod_gate)


def _ffn_down_kernel(*refs, tm, tk, final):
    (a_ref, v_ref, ap_ref, vp_ref, an_ref, vn_ref, cw_ref, cb_ref, w_ref, x_ref, mg_ref) = refs[:11]
    o_ref, ext_ref, hid_ref = refs[-3:]
    i = pl.program_id(1)
    last_i = pl.num_programs(1) - 1

    def conv(slot, c_ref, p_ref, n_ref, c0, w0):
        ext_ref[slot, 0:HALO, :] = jnp.where(i == 0, 0.0, p_ref[0, :, c0:c0 + tk].astype(F32))
        ext_ref[slot, HALO:HALO + tm, :] = c_ref[0, :, c0:c0 + tk].astype(F32)
        ext_ref[slot, HALO + tm:2 * HALO + tm, :] = jnp.where(
            i == last_i, 0.0, n_ref[0, :, c0:c0 + tk].astype(F32))
        w = cw_ref[:, w0:w0 + tk]
        return (w[0:1] * ext_ref[slot, HALO - 1:HALO - 1 + tm, :]
                + w[1:2] * ext_ref[slot, HALO:HALO + tm, :]
                + w[2:3] * ext_ref[slot, HALO + 1:HALO + 1 + tm, :] + cb_ref[:, w0:w0 + tk])

    out = None
    for kc in range(D_FF // tk):
        c0 = kc * tk
        a = conv((2 * kc) % 4, a_ref, ap_ref, an_ref, c0, c0)
        v = conv((2 * kc + 1) % 4, v_ref, vp_ref, vn_ref, c0, D_FF + c0)
        hid_ref[kc % 2] = (a * jax.nn.sigmoid(a) * v).astype(BF16)
        part = jnp.dot(hid_ref[kc % 2], w_ref[c0:c0 + tk, :], preferred_element_type=F32)
        out = part if out is None else out + part
    out = x_ref[0] + mg_ref[0] * out
    if final:
        out = _rms(out, refs[11][...])
    o_ref[0] = out


def ffn_down(u, conv_w, conv_b, w_down, layer, x, mod_gate, final_g=None):
    b, t, d = x.shape
    tk = 512
    tm = min(256, t)
    hb = tm // HALO
    n_hb = t // HALO
    per_batch = mod_gate.shape[0] > 1
    main = lambda cb: pl.BlockSpec((1, tm, D_FF), lambda bi, i: (bi, i, cb))
    prev = lambda cb: pl.BlockSpec((1, HALO, D_FF), lambda bi, i: (bi, jnp.maximum(i * hb - 1, 0), cb))
    nxt = lambda cb: pl.BlockSpec((1, HALO, D_FF),
                                  lambda bi, i: (bi, jnp.minimum((i + 1) * hb, n_hb - 1), cb))
    const = lambda shape: pl.BlockSpec(shape, lambda bi, i: (0,) * len(shape))
    in_specs = [main(0), main(1), prev(0), prev(1), nxt(0), nxt(1),
                const((3, 2 * D_FF)), const((1, 2 * D_FF)),
                pl.BlockSpec((None, D_FF, d), lambda bi, i: (layer, 0, 0),
                             pipeline_mode=pl.Buffered(1)),
                pl.BlockSpec((1, tm, d), lambda bi, i: (bi, i, 0)),
                pl.BlockSpec((1, 1, d), (lambda bi, i: (bi, 0, 0)) if per_batch
                             else (lambda bi, i: (0, 0, 0)))]
    args = [u, u, u, u, u, u, conv_w, conv_b.reshape(1, 2 * D_FF), w_down, x, mod_gate]
    if final_g is not None:
        in_specs.append(const((1, d)))
        args.append(final_g.reshape(1, d))
    kern = functools.partial(_ffn_down_kernel, tm=tm, tk=tk, final=final_g is not None)
    return pl.pallas_call(
        kern,
        out_shape=jax.ShapeDtypeStruct((b, t, d), F32),
        grid=(b, t // tm),
        in_specs=in_specs,
        out_specs=pl.BlockSpec((1, tm, d), lambda bi, i: (bi, i, 0)),
        scratch_shapes=[pltpu.VMEM((4, tm + 2 * HALO, tk), F32),
                        pltpu.VMEM((2, tm, tk), BF16)],
        compiler_params=_params(2),
        name="ffn_down",
    )(*args)


def mixer_sublayer(x, y, y_c, y_lat, rope, p, mod_gate, lam_init):
    ya = depthwise_conv(y, COL_A, W_CONV, p['conv_a_w'], p['conv_a_b'], glu=True,
                        ln=(p['ln_a_g'], p['ln_a_b']), out_dtype=BF16, name="conformer_conv")
    u = depthwise_conv(y, COL_B, 3 * W_HYENA, p['short_b_w'], p['short_b_b'], name="hyena_short_conv")
    filt = (p['filt_w1'], p['filt_b1'], p['filt_w2'], p['filt_b2'], p['filt_w3'])
    zb = hyena_long_conv(u, p['hyena_skip'], filt)
    oc = diff_attention(y, y_c, y_lat, rope, p['diff_lambda'], p['subln_g'], lam_init)
    return gated_merge(ya, zb, oc, y, p['b_gate'], p['w_a_out'], p['w_b_out'], p['w_c_out'],
                       p['w_o'], x, mod_gate)


def kernel(x, c, ctx, c_ctx, w_ada, b_ada, norm1_g, norm2_g, w_in, b_gate, conv_a_w, conv_a_b, ln_a_g, ln_a_b, w_a_out, short_b_w, short_b_b, filt_w1, filt_b1, filt_w2, filt_b2, filt_w3, hyena_skip, w_b_out, diff_lambda, subln_g, w_c_out, w_o, w_up, conv_f_w, conv_f_b, w_down, final_g):
    batch = x.shape[0]
    rope = rope_tables(x.shape[1])
    cc = jnp.concatenate([c, c_ctx[None, :], jnp.zeros((8 - batch - 1, D_MODEL), F32)], axis=0)
    gate_rot = N_MIX // TN
    w_in_b, w_up_b, w_down_b = w_in.astype(BF16), w_up.astype(BF16), w_down.astype(BF16)
    for l in range(DEPTH):
        p = {
            'b_gate': b_gate[l],
            'conv_a_w': conv_a_w[l], 'conv_a_b': conv_a_b[l],
            'ln_a_g': ln_a_g[l], 'ln_a_b': ln_a_b[l], 'w_a_out': w_a_out[l],
            'short_b_w': short_b_w[l], 'short_b_b': short_b_b[l],
            'filt_w1': filt_w1[l], 'filt_b1': filt_b1[l], 'filt_w2': filt_w2[l],
            'filt_b2': filt_b2[l], 'filt_w3': filt_w3[l],
            'hyena_skip': hyena_skip[l], 'w_b_out': w_b_out[l],
            'diff_lambda': diff_lambda[l], 'subln_g': subln_g[l], 'w_c_out': w_c_out[l],
            'w_o': w_o[l],
        }
        update_ctx = l < DEPTH - 1
        last = l == DEPTH - 1
        lam_init = 0.8 - 0.6 * math.exp(-0.3 * l)
        mod = ada_projection(cc, w_ada, l, b_ada[l])
        mx = [mod[:batch, None, j * D_MODEL:(j + 1) * D_MODEL] for j in range(6)]
        mc = [mod[batch:batch + 1, None, j * D_MODEL:(j + 1) * D_MODEL] for j in range(6)]

        y_x = norm_mod_matmul(x, norm1_g[l], mx[0], mx[1], w_in_b, l, gate_rot, "mm_in_x")
        y_c = norm_mod_matmul(ctx, norm1_g[l], mc[0], mc[1], w_in_b, l, gate_rot, "mm_in_c")
        x = mixer_sublayer(x, y_x, y_c, y_x, rope, p, mx[2], lam_init)
        u = norm_mod_matmul(x, norm2_g[l], mx[3], mx[4], w_up_b, l, 0, "mm_up_x")
        x = ffn_down(u, conv_f_w[l], conv_f_b[l], w_down_b, l, x, mx[5], final_g if last else None)
        if update_ctx:
            ctx = mixer_sublayer(ctx, y_c, y_c, None, None, p, mc[2], lam_init)
            u = norm_mod_matmul(ctx, norm2_g[l], mc[3], mc[4], w_up_b, l, 0, "mm_up_c")
            ctx = ffn_down(u, conv_f_w[l], conv_f_b[l], w_down_b, l, ctx, mc[5])
    return x
```

```python
import functools
import math

import numpy as np
import jax
import jax.numpy as jnp
from jax import lax
from jax.experimental import pallas as pl
from jax.experimental.pallas import tpu as pltpu

D_MODEL = 2048
DEPTH = 2
GRID_W = 64
W_CONV = D_MODEL // 4
W_HYENA = D_MODEL // 4
N_HEADS = 8
HEAD_DIM = 64
HEAD_W = 2 * HEAD_DIM
W_DIFF = N_HEADS * HEAD_W
D_FF = 5632
HYENA_ORDER = 2
HYENA_BANDS = 8
HYENA_TARGET = 1e-2
HYENA_FAST_PCT = 0.3
HYENA_SLOW_PCT = 1.5
ROPE_BASE = 10000.0
EPS = 1e-6
LOG2E = 1.4426950408889634

LANE = 128
HALO = 16
FREQ_BLOCK = 256
TN = 512
VMEM_LIMIT_BYTES = 56 * 1024 * 1024

N_IN = 2 * W_CONV + 3 * W_HYENA + 3 * W_DIFF + 3 * D_MODEL
N_MIX = N_IN - 3 * D_MODEL
COL_GATE = 0
COL_A = 3 * D_MODEL
COL_B = COL_A + 2 * W_CONV
COL_Q = COL_B + 3 * W_HYENA
COL_K = COL_Q + W_DIFF
COL_V = COL_K + W_DIFF

F32 = jnp.float32
BF16 = jnp.bfloat16


def _params(n_axes):
    return pltpu.CompilerParams(
        dimension_semantics=("arbitrary",) * n_axes, vmem_limit_bytes=VMEM_LIMIT_BYTES)


def _rms(x, g):
    return x * lax.rsqrt(jnp.mean(x * x, axis=-1, keepdims=True) + EPS) * g


def _ada_kernel(c_ref, w_ref, b_ref, o_ref):
    c = c_ref[...]
    h = c * jax.nn.sigmoid(c)
    o_ref[...] = jnp.dot(h, w_ref[...], precision=lax.Precision.HIGHEST,
                         preferred_element_type=F32) + b_ref[...]


def ada_projection(cc, w, layer, b):
    m, d = cc.shape
    n = w.shape[2]
    tn = 1024
    return pl.pallas_call(
        _ada_kernel,
        out_shape=jax.ShapeDtypeStruct((m, n), F32),
        grid=(n // tn,),
        in_specs=[pl.BlockSpec((m, d), lambda j: (0, 0)),
                  pl.BlockSpec((None, d, tn), lambda j: (layer, 0, j)),
                  pl.BlockSpec((1, tn), lambda j: (0, j))],
        out_specs=pl.BlockSpec((m, tn), lambda j: (0, j)),
        compiler_params=_params(1),
        name="ada_proj",
    )(cc, w, b.reshape(1, n))


def _nmm_kernel(x_ref, g_ref, sh_ref, sc_ref, w_ref, o_ref, h_ref):
    @pl.when(pl.program_id(2) == 0)
    def _():
        rb = min(256, h_ref.shape[0])

        def rows(r, carry):
            r0 = pl.multiple_of(r * rb, rb)
            y = _rms(x_ref[0, pl.ds(r0, rb), :], g_ref[...])
            h_ref[pl.ds(r0, rb), :] = (y * (1.0 + sc_ref[0]) + sh_ref[0]).astype(BF16)
            return carry

        lax.fori_loop(0, h_ref.shape[0] // rb, rows, 0)

    o_ref[0] = jnp.dot(h_ref[...], w_ref[...], preferred_element_type=F32).astype(o_ref.dtype)


def norm_mod_matmul(x, g, shift, scale, w, layer, col_rot, name):
    b, t, d = x.shape
    n = w.shape[2]
    tm = min(2048, t)
    nt = n // TN
    per_batch = shift.shape[0] > 1
    mod_spec = pl.BlockSpec((1, 1, d), (lambda bi, i, j: (bi, 0, 0)) if per_batch
                            else (lambda bi, i, j: (0, 0, 0)))
    return pl.pallas_call(
        _nmm_kernel,
        out_shape=jax.ShapeDtypeStruct((b, t, n), BF16),
        grid=(b, t // tm, nt),
        in_specs=[pl.BlockSpec((1, tm, d), lambda bi, i, j: (bi, i, 0)),
                  pl.BlockSpec((1, d), lambda bi, i, j: (0, 0)),
                  mod_spec, mod_spec,
                  pl.BlockSpec((None, d, TN), lambda bi, i, j: (layer, 0, (j + col_rot) % nt))],
        out_specs=pl.BlockSpec((1, tm, TN), lambda bi, i, j: (bi, i, j)),
        scratch_shapes=[pltpu.VMEM((tm, d), BF16)],
        compiler_params=_params(3),
        name=name,
    )(x, g.reshape(1, d), shift, scale, w)


def _dwconv_kernel(*refs, taps, glu, post, tt, rb):
    n_in = 2 if glu else 1
    main = refs[:n_in]
    prev = refs[n_in:2 * n_in]
    nxt = refs[2 * n_in:3 * n_in]
    rest = refs[3 * n_in:]
    if post:
        w_ref, b_ref, lg_ref, lb_ref, o_ref, ext_ref = rest
    else:
        w_ref, b_ref, o_ref, ext_ref = rest
    i = pl.program_id(1)
    last = pl.num_programs(1) - 1

    def value(blocks):
        a = blocks[0][0].astype(F32)
        if glu:
            a = a * jax.nn.sigmoid(blocks[1][0].astype(F32))
        return a

    ext_ref[0:HALO, :] = jnp.where(i == 0, 0.0, value(prev))
    ext_ref[HALO:HALO + tt, :] = value(main)
    ext_ref[HALO + tt:2 * HALO + tt, :] = jnp.where(i == last, 0.0, value(nxt))

    pad = (taps - 1) // 2
    for r in range(tt // rb):
        acc = jnp.zeros((rb, ext_ref.shape[1]), F32) + b_ref[...]
        for j in range(taps):
            acc = acc + w_ref[j:j + 1, :] * ext_ref[pl.ds(HALO - pad + j + r * rb, rb), :]
        if post:
            mu = jnp.mean(acc, axis=-1, keepdims=True)
            cen = acc - mu
            var = jnp.mean(cen * cen, axis=-1, keepdims=True)
            y = cen * lax.rsqrt(var + EPS) * lg_ref[...] + lb_ref[...]
            acc = y * jax.nn.sigmoid(y)
        o_ref[0, r * rb:(r + 1) * rb, :] = acc.astype(o_ref.dtype)


def depthwise_conv(y, col, width, w, bias, glu=False, ln=None, out_dtype=F32, name="dwconv"):
    b, t, _ = y.shape
    taps = w.shape[0]
    tt = min(512, t)
    cw = width if ln is not None else min(512, width)
    ncb = width // cw
    cb0 = col // cw
    hb = tt // HALO
    n_hb = t // HALO
    cols = [cb0, cb0 + ncb] if glu else [cb0]

    def main_spec(c0):
        return pl.BlockSpec((1, tt, cw), lambda bi, i, c: (bi, i, c0 + c))

    def prev_spec(c0):
        return pl.BlockSpec((1, HALO, cw), lambda bi, i, c: (bi, jnp.maximum(i * hb - 1, 0), c0 + c))

    def next_spec(c0):
        return pl.BlockSpec((1, HALO, cw),
                            lambda bi, i, c: (bi, jnp.minimum((i + 1) * hb, n_hb - 1), c0 + c))

    vec = lambda rows: pl.BlockSpec((rows, cw), lambda bi, i, c: (0, c))
    in_specs = ([main_spec(c0) for c0 in cols] + [prev_spec(c0) for c0 in cols]
                + [next_spec(c0) for c0 in cols] + [vec(taps), vec(1)])
    args = [y] * (3 * len(cols)) + [w, bias.reshape(1, width)]
    if ln is not None:
        in_specs += [vec(1), vec(1)]
        args += [ln[0].reshape(1, width), ln[1].reshape(1, width)]
    kern = functools.partial(_dwconv_kernel, taps=taps, glu=glu, post=ln is not None, tt=tt,
                             rb=min(128, tt))
    return pl.pallas_call(
        kern,
        out_shape=jax.ShapeDtypeStruct((b, t, width), out_dtype),
        grid=(b, t // tt, ncb),
        in_specs=in_specs,
        out_specs=pl.BlockSpec((1, tt, cw), lambda bi, i, c: (bi, i, c)),
        scratch_shapes=[pltpu.VMEM((tt + 2 * HALO, cw), F32)],
        compiler_params=_params(3),
        name=name,
    )(*args)


def _swap_halves(x):
    lanes = lax.broadcasted_iota(jnp.int32, x.shape, 1)
    first = (lanes % HEAD_DIM) < (HEAD_DIM // 2)
    return jnp.where(first, pltpu.roll(x, LANE - HEAD_DIM // 2, 1), pltpu.roll(x, HEAD_DIM // 2, 1))


def _attn_kernel(*refs, lam_init, n_sub, t_c, has_lat):
    if has_lat:
        (q_ref, kc_ref, vc_ref, kx_ref, vx_ref, cq_ref, sq_ref, ck_ref, sk_ref, dl_ref, g_ref,
         o_ref, s_ref, p_ref, k_s, vt_s) = refs
    else:
        q_ref, kc_ref, vc_ref, dl_ref, g_ref, o_ref, s_ref, p_ref, k_s, vt_s = refs
    t_k = k_s.shape[0]
    tq = q_ref.shape[1]
    ts = tq // n_sub
    ck = 256

    @pl.when(pl.program_id(2) == 0)
    def _():
        k_s[0:t_c, :] = kc_ref[0]
        if has_lat:
            kx = kx_ref[0].astype(F32)
            k_s[t_c:, :] = (kx * ck_ref[...] + _swap_halves(kx) * sk_ref[...]).astype(BF16)
        for c in range(t_k // ck):
            lo = c * ck
            src = vc_ref[0, lo:lo + ck, :] if lo < t_c else vx_ref[0, lo - t_c:lo - t_c + ck, :]
            vt_s[:, lo:lo + ck] = src.astype(F32).T.astype(BF16)

    q = q_ref[0].astype(F32)
    if has_lat:
        q = q * cq_ref[...] + _swap_halves(q) * sq_ref[...]
    qt = (q * (HEAD_DIM ** -0.5 * LOG2E)).T
    rows = lax.broadcasted_iota(jnp.int32, qt.shape, 0)
    qts = (jnp.where(rows < HEAD_DIM, qt, 0.0).astype(BF16),
           jnp.where(rows >= HEAD_DIM, qt, 0.0).astype(BF16))

    kk = k_s[...]
    vt = vt_s[...]
    streams = [(mi, si) for si in range(n_sub) for mi in range(2)]
    ms = {}
    for mi, si in streams:
        s = jnp.dot(kk, qts[mi][:, si * ts:(si + 1) * ts], preferred_element_type=F32)
        s_ref[mi, si] = s
        ms[mi, si] = jnp.max(s, axis=0, keepdims=True)
    outs = {}
    for mi, si in streams:
        p = jnp.exp2(s_ref[mi, si] - ms[mi, si])
        l = jnp.sum(p, axis=0, keepdims=True)
        p_ref[mi, si] = p.astype(BF16)
        acc = jnp.dot(vt, p_ref[mi, si], preferred_element_type=F32)
        outs[mi, si] = acc / l

    dl = dl_ref[...]
    s1 = jnp.sum(dl[0:1] * dl[1:2], axis=-1, keepdims=True)
    s2 = jnp.sum(dl[2:3] * dl[3:4], axis=-1, keepdims=True)
    lam = jnp.exp(s1) - jnp.exp(s2) + lam_init
    for si in range(n_sub):
        ot = outs[0, si] - lam * outs[1, si]
        ot = ot * lax.rsqrt(jnp.mean(ot * ot, axis=0, keepdims=True) + EPS)
        o = ot.T
        o_ref[0, si * ts:(si + 1) * ts, :] = (o * g_ref[0] * (1.0 - lam_init)).astype(o_ref.dtype)


def diff_attention(y_q, y_c, y_x, rope, diff_lambda, subln_g, lam_init):
    b, t_q, _ = y_q.shape
    t_c = y_c.shape[1]
    has_lat = y_x is not None
    t_k = t_c + (y_x.shape[1] if has_lat else 0)
    tq = min(512, t_q)
    ts = min(256, tq)
    n_sub = tq // ts
    qb, kb, vb = COL_Q // HEAD_W, COL_K // HEAD_W, COL_V // HEAD_W
    head = lambda base, rows: pl.BlockSpec((1, rows, HEAD_W), lambda bi, hi, i: (bi, 0, base + hi))
    in_specs = [pl.BlockSpec((1, tq, HEAD_W), lambda bi, hi, i: (bi, i, qb + hi)),
                head(kb, t_c), head(vb, t_c)]
    args = [y_q, y_c, y_c]
    if has_lat:
        t_x = y_x.shape[1]
        in_specs += [head(kb, t_x), head(vb, t_x),
                     pl.BlockSpec((tq, HEAD_W), lambda bi, hi, i: (i, 0)),
                     pl.BlockSpec((tq, HEAD_W), lambda bi, hi, i: (i, 0)),
                     pl.BlockSpec((t_x, HEAD_W), lambda bi, hi, i: (0, 0)),
                     pl.BlockSpec((t_x, HEAD_W), lambda bi, hi, i: (0, 0))]
        args += [y_x, y_x, rope[0], rope[1], rope[0], rope[1]]
    in_specs += [pl.BlockSpec((4, HEAD_DIM), lambda bi, hi, i: (0, 0)),
                 pl.BlockSpec((1, 1, HEAD_W), lambda bi, hi, i: (hi, 0, 0))]
    args += [diff_lambda.astype(F32), subln_g.astype(F32).reshape(N_HEADS, 1, HEAD_W)]
    kern = functools.partial(_attn_kernel, lam_init=lam_init, n_sub=n_sub, t_c=t_c, has_lat=has_lat)
    return pl.pallas_call(
        kern,
        out_shape=jax.ShapeDtypeStruct((b, t_q, W_DIFF), BF16),
        grid=(b, N_HEADS, t_q // tq),
        in_specs=in_specs,
        out_specs=pl.BlockSpec((1, tq, HEAD_W), lambda bi, hi, i: (bi, i, hi)),
        scratch_shapes=[pltpu.VMEM((2, n_sub, t_k, ts), F32),
                        pltpu.VMEM((2, n_sub, t_k, ts), BF16),
                        pltpu.VMEM((t_k, HEAD_W), BF16),
                        pltpu.VMEM((HEAD_W, t_k), BF16)],
        compiler_params=_params(3),
        name="diff_attn",
    )(*args)


def rope_tables(t):
    rows = t // GRID_W
    row = np.repeat(np.arange(rows), GRID_W).astype(np.float64)
    col = np.tile(np.arange(GRID_W), rows).astype(np.float64)
    n_freq = HEAD_DIM // 4
    inv = (ROPE_BASE ** (-np.arange(n_freq, dtype=np.float32) / n_freq)).astype(np.float64)
    ang = np.concatenate([row[:, None] * inv, col[:, None] * inv], axis=-1)
    c, s = np.cos(ang), np.sin(ang)
    cos_t = np.tile(np.concatenate([c, c], axis=-1), (1, 2))
    sin_t = np.tile(np.concatenate([-s, s], axis=-1), (1, 2))
    return jnp.asarray(cos_t, F32), jnp.asarray(sin_t, F32)


@functools.lru_cache(maxsize=None)
def _dft_factors(length):
    n = 2 * length
    fb = min(FREQ_BLOCK, length)
    nb = length // fb
    kk = np.arange(length, dtype=np.int64).reshape(nb, 1, fb)
    k_rows = np.concatenate([kk, kk], axis=1)
    is_sin = np.zeros((nb, 2, fb), bool)
    is_sin[:, 1, :] = True
    k_rows[0, 1, 0] = length
    is_sin[0, 1, 0] = False
    k_rows = k_rows.reshape(-1)
    is_sin = is_sin.reshape(-1)
    n_hi = max(length // LANE, 1)
    n_lo = min(LANE, length)
    a_lo = 2.0 * np.pi * ((k_rows[:, None] * np.arange(n_lo)[None, :]) % n) / n
    a_hi = 2.0 * np.pi * ((k_rows[:, None] * (LANE * np.arange(n_hi))[None, :]) % n) / n
    c_lo, s_lo, c_hi, s_hi = np.cos(a_lo), np.sin(a_lo), np.cos(a_hi), np.sin(a_hi)
    p = np.where(is_sin[:, None], s_hi, c_hi)
    q = np.where(is_sin[:, None], c_hi, -s_hi)
    scale = np.full((2 * length,), 2.0 / n)
    scale[0] = 1.0 / n
    scale[fb] = 1.0 / n
    f32 = lambda a: np.asarray(a, np.float32)
    return f32(p), f32(q), f32(c_lo), f32(s_lo), f32(scale.reshape(-1, 1))


def dft_tables(length):
    p, q, c_lo, s_lo, _ = _dft_factors(length)
    fwd = (jnp.asarray(p)[:, :, None] * jnp.asarray(c_lo)[:, None, :]
           + jnp.asarray(q)[:, :, None] * jnp.asarray(s_lo)[:, None, :])
    fwd = fwd.reshape(2 * length, length).astype(BF16)
    return fwd, fwd.T


def _cmul(za, zb, ha, hb, first_rows):
    ya = za * ha - zb * hb
    yb = za * hb + zb * ha
    ya = jnp.where(first_rows, za * ha, ya)
    yb = jnp.where(first_rows, zb * hb, yb)
    return ya, yb


def _fwd_kernel(f_ref, z_ref, h_ref, c_ref, y_ref, zb_ref, *, fb):
    @pl.when(pl.program_id(1) == 0)
    def _():
        zb_ref[...] = z_ref[0].astype(BF16)

    spec = jnp.dot(f_ref[...], zb_ref[...], preferred_element_type=F32)
    za, zb = spec[:fb], spec[fb:]
    h = h_ref[...]
    rows = lax.broadcasted_iota(jnp.int32, za.shape, 0)
    first = jnp.logical_and(rows == 0, pl.program_id(1) == 0)
    ya, yb = _cmul(za, zb, h[:fb], h[fb:], first)
    c = c_ref[...]
    y_ref[0, :fb, :] = (ya * c[:fb]).astype(y_ref.dtype)
    y_ref[0, fb:, :] = (yb * c[fb:]).astype(y_ref.dtype)


def hyena_forward(z, z_col, hspec, order, fwd, length):
    b = z.shape[0]
    w = W_HYENA
    scale = jnp.asarray(_dft_factors(length)[4])
    fb = min(FREQ_BLOCK, length)
    kern = functools.partial(_fwd_kernel, fb=fb)
    return pl.pallas_call(
        kern,
        out_shape=jax.ShapeDtypeStruct((b, 2 * length, w), BF16),
        grid=(b, length // fb),
        in_specs=[pl.BlockSpec((2 * fb, length), lambda bi, i: (i, 0)),
                  pl.BlockSpec((1, length, w), lambda bi, i: (bi, 0, z_col)),
                  pl.BlockSpec((2 * fb, w), lambda bi, i: (i, order)),
                  pl.BlockSpec((2 * fb, 1), lambda bi, i: (i, 0))],
        out_specs=pl.BlockSpec((1, 2 * fb, w), lambda bi, i: (bi, i, 0)),
        scratch_shapes=[pltpu.VMEM((length, w), BF16)],
        compiler_params=_params(2),
        name="hyena_fwd",
    )(fwd, z, hspec, scale)


def _inv_kernel(g_ref, y_ref, z_ref, gate_ref, skip_ref, o_ref):
    conv = jnp.dot(g_ref[...], y_ref[0], preferred_element_type=F32)
    o_ref[0] = (gate_ref[0] * (conv + z_ref[0] * skip_ref[...])).astype(o_ref.dtype)


def hyena_inverse(y, z, z_col, gate, gate_col, skip, inv, length, out_dtype):
    b = z.shape[0]
    w = W_HYENA
    tm = min(512, length)
    return pl.pallas_call(
        _inv_kernel,
        out_shape=jax.ShapeDtypeStruct((b, length, w), out_dtype),
        grid=(b, length // tm),
        in_specs=[pl.BlockSpec((tm, 2 * length), lambda bi, i: (i, 0)),
                  pl.BlockSpec((1, 2 * length, w), lambda bi, i: (bi, 0, 0)),
                  pl.BlockSpec((1, tm, w), lambda bi, i: (bi, i, z_col)),
                  pl.BlockSpec((1, tm, w), lambda bi, i: (bi, i, gate_col)),
                  pl.BlockSpec((1, w), lambda bi, i: (0, 0))],
        out_specs=pl.BlockSpec((1, tm, w), lambda bi, i: (bi, i, 0)),
        compiler_params=_params(2),
        name="hyena_inv",
    )(inv, y, z, gate, skip.reshape(1, w))


def _filt_kernel(feat_ref, w1_ref, b1_ref, w2_ref, b2_ref, w3f_ref, w3b_ref, dl_ref,
                 taps_ref, hf_ref, hb_ref, *, rb):
    hi = lax.Precision.HIGHEST
    length, ct = hf_ref.shape
    n_blocks = length // rb

    def taps(i, norm):
        r0 = pl.multiple_of(i * rb, rb)
        feats = feat_ref[pl.ds(r0, rb), :]
        h = jnp.sin(jnp.dot(feats, w1_ref[...], precision=hi, preferred_element_type=F32)
                    + b1_ref[...])
        h = jnp.sin(jnp.dot(h, w2_ref[...], precision=hi, preferred_element_type=F32)
                    + b2_ref[...])
        decay = jnp.exp(-feats[:, 0:1] * dl_ref[...])
        hf = jnp.dot(h, w3f_ref[...], precision=hi, preferred_element_type=F32) * decay
        hb = jnp.dot(h, w3b_ref[...], precision=hi, preferred_element_type=F32) * decay
        rows = lax.broadcasted_iota(jnp.int32, hb.shape, 0) + r0
        hb = jnp.where(rows == 0, 0.0, hb)
        hf_ref[pl.ds(r0, rb), :] = hf
        hb_ref[pl.ds(r0, rb), :] = hb
        return (norm + jnp.sum(jnp.abs(hf), axis=0, keepdims=True)
                + jnp.sum(jnp.abs(hb), axis=0, keepdims=True))

    norm = lax.fori_loop(0, n_blocks, taps, jnp.zeros((1, ct), F32))
    inv = 1.0 / norm

    def normalise(i, carry):
        r0 = pl.multiple_of(i * rb, rb)
        taps_ref[0, pl.ds(r0, rb), :] = hf_ref[pl.ds(r0, rb), :] * inv
        taps_ref[1, pl.ds(r0, rb), :] = hb_ref[pl.ds(r0, rb), :] * inv
        return carry

    lax.fori_loop(0, n_blocks, normalise, 0)


def hyena_filter_taps(length, filt_w1, filt_b1, filt_w2, filt_b2, filt_w3):
    w = W_HYENA
    ct = 256
    pos = np.arange(length, dtype=np.float64)
    t = pos / max(length - 1, 1)
    bands = np.linspace(1e-4, HYENA_BANDS - 1, HYENA_BANDS)
    ang = (2.0 * math.pi * pos / length)[:, None] * bands[None, :]
    feats = np.concatenate([t[:, None], np.cos(ang), -np.sin(ang)], axis=-1).astype(np.float32)
    deltas = np.abs(np.linspace(math.log(HYENA_TARGET) / HYENA_SLOW_PCT,
                                math.log(HYENA_TARGET) / HYENA_FAST_PCT, w)).astype(np.float32)
    emb, nf = filt_w1.shape
    npc = w // ct
    full = lambda shape: pl.BlockSpec(shape, lambda o, c: (0,) * len(shape))
    kern = functools.partial(_filt_kernel, rb=min(256, length))
    return pl.pallas_call(
        kern,
        out_shape=jax.ShapeDtypeStruct((HYENA_ORDER, 2, length, w), F32),
        grid=(HYENA_ORDER, npc),
        in_specs=[full((length, emb)), full((emb, nf)), full((1, nf)), full((nf, nf)), full((1, nf)),
                  pl.BlockSpec((nf, ct), lambda o, c: (0, o * 2 * npc + c)),
                  pl.BlockSpec((nf, ct), lambda o, c: (0, o * 2 * npc + npc + c)),
                  pl.BlockSpec((1, ct), lambda o, c: (0, c))],
        out_specs=pl.BlockSpec((None, 2, length, ct), lambda o, c: (o, 0, 0, c)),
        scratch_shapes=[pltpu.VMEM((length, ct), F32),
                        pltpu.VMEM((length, ct), F32)],
        compiler_params=_params(2),
        name="hyena_filt",
    )(jnp.asarray(feats), filt_w1, filt_b1.reshape(1, nf), filt_w2, filt_b2.reshape(1, nf),
      filt_w3, filt_w3, jnp.asarray(deltas).reshape(1, w))


def _hspec_kernel(f_ref, t_ref, o_ref, *, fb):
    f = f_ref[...]
    hs = (t_ref[0] + t_ref[1]).astype(BF16)
    hd = (t_ref[0] - t_ref[1]).astype(BF16)
    a = jnp.dot(f, hs, preferred_element_type=F32)
    bd = jnp.dot(f[fb:], hd, preferred_element_type=F32)
    rows = lax.broadcasted_iota(jnp.int32, bd.shape, 0)
    first = jnp.logical_and(rows == 0, pl.program_id(1) == 0)
    o_ref[:fb, :] = a[:fb]
    o_ref[fb:, :] = jnp.where(first, a[fb:], bd)


def hyena_filter_spectrum(taps, fwd, length):
    fb = min(FREQ_BLOCK, length)
    w = W_HYENA
    kern = functools.partial(_hspec_kernel, fb=fb)
    return pl.pallas_call(
        kern,
        out_shape=jax.ShapeDtypeStruct((2 * length, HYENA_ORDER * w), F32),
        grid=(HYENA_ORDER, length // fb),
        in_specs=[pl.BlockSpec((2 * fb, length), lambda j, i: (i, 0)),
                  pl.BlockSpec((None, 2, length, w), lambda j, i: (j, 0, 0, 0))],
        out_specs=pl.BlockSpec((2 * fb, w), lambda j, i: (i, j)),
        compiler_params=_params(2),
        name="hyena_hspec",
    )(fwd, taps)


FFT_N1 = 64
FFT_N2 = 128
FFT_N = FFT_N1 * FFT_N2
FFT_KB = 8
FFT_TN = 8192


@functools.lru_cache(maxsize=None)
def _fft_tables():
    n1h = FFT_N1 // 2
    k1 = np.arange(FFT_N1)
    ang = 2.0 * np.pi * np.outer(k1, np.arange(n1h)) / FFT_N1
    c, s = np.cos(ang), np.sin(ang)
    f1 = np.block([[c, s], [-s, c]])
    f1i = np.block([[c.T, -s.T], [s.T, c.T]]) / FFT_N
    k2 = np.arange(FFT_N2)
    n2 = np.arange(FFT_N2)
    kfull = k1[:, None, None] + FFT_N1 * k2[None, :, None]
    ang2 = 2.0 * np.pi * ((kfull * n2[None, None, :]) % FFT_N) / FFT_N
    gc, gs = np.cos(ang2), np.sin(ang2)
    g = np.concatenate([np.concatenate([gc, gs], axis=2),
                        np.concatenate([-gs, gc], axis=2)], axis=1)
    f32 = lambda a: np.asarray(a, np.float32)
    return f32(f1), f32(g), f32(np.transpose(g, (0, 2, 1))), f32(f1i)


def _fft_s1_kernel(f_ref, x_ref, o_ref):
    o_ref[0] = jnp.dot(f_ref[...], x_ref[0].astype(BF16),
                       preferred_element_type=F32).astype(o_ref.dtype)


def fft_stage1(x):
    p, rows, cols = x.shape
    f1 = jnp.asarray(_fft_tables()[0][:, :rows], BF16)
    return pl.pallas_call(
        _fft_s1_kernel,
        out_shape=jax.ShapeDtypeStruct((p, 2 * FFT_N1, cols), BF16),
        grid=(p, cols // FFT_TN),
        in_specs=[pl.BlockSpec((2 * FFT_N1, rows), lambda pi, j: (0, 0)),
                  pl.BlockSpec((1, rows, FFT_TN), lambda pi, j: (pi, 0, j))],
        out_specs=pl.BlockSpec((1, 2 * FFT_N1, FFT_TN), lambda pi, j: (pi, 0, j)),
        compiler_params=_params(2),
        name="fft_s1",
    )(f1, x)


def _fft_s2_kernel(*refs, with_filter):
    if with_filter:
        g_ref, a_ref, hp_ref, hq_ref, y_ref = refs
    else:
        g_ref, a_ref, y_ref = refs
    half = FFT_N2
    for j in range(FFT_KB):
        rhs = jnp.concatenate([a_ref[0, 0, j], a_ref[0, 1, j]], axis=0)
        s = jnp.dot(g_ref[j], rhs, preferred_element_type=F32)
        if with_filter:
            hre = hp_ref[0, j, :half, :] + hq_ref[0, j, :half, :]
            him = hp_ref[0, j, half:, :] - hq_ref[0, j, half:, :]
            sre, sim = s[:half], s[half:]
            y_ref[0, j, :half, :] = (sre * hre - sim * him).astype(y_ref.dtype)
            y_ref[0, j, half:, :] = (sre * him + sim * hre).astype(y_ref.dtype)
        else:
            y_ref[0, j] = s.astype(y_ref.dtype)


def fft_stage2(a, hspec=None, order=0):
    p = a.shape[0]
    w = W_HYENA
    g = jnp.asarray(_fft_tables()[1], BF16)
    a5 = a.reshape(p, 2, FFT_N1, FFT_N2, w)
    with_filter = hspec is not None
    in_specs = [pl.BlockSpec((FFT_KB, 2 * FFT_N2, 2 * FFT_N2), lambda pi, j: (j, 0, 0)),
                pl.BlockSpec((1, 2, FFT_KB, FFT_N2, w), lambda pi, j: (pi, 0, j, 0, 0))]
    args = [g, a5]
    if with_filter:
        in_specs += [pl.BlockSpec((1, FFT_KB, 2 * FFT_N2, w), lambda pi, j: (2 * order, j, 0, 0)),
                     pl.BlockSpec((1, FFT_KB, 2 * FFT_N2, w), lambda pi, j: (2 * order + 1, j, 0, 0))]
        args += [hspec, hspec]
    return pl.pallas_call(
        functools.partial(_fft_s2_kernel, with_filter=with_filter),
        out_shape=jax.ShapeDtypeStruct((p, FFT_N1, 2 * FFT_N2, w), BF16 if with_filter else F32),
        grid=(p, FFT_N1 // FFT_KB),
        in_specs=in_specs,
        out_specs=pl.BlockSpec((1, FFT_KB, 2 * FFT_N2, w), lambda pi, j: (pi, j, 0, 0)),
        compiler_params=_params(2),
        name="fft_s2",
    )(*args)


def _ifft_a_kernel(gt_ref, y_ref, b_ref):
    half = FFT_N2
    for j in range(FFT_KB):
        b = jnp.dot(gt_ref[j], y_ref[0, j], preferred_element_type=F32)
        b_ref[0, 0, j] = b[:half].astype(b_ref.dtype)
        b_ref[0, 1, j] = b[half:].astype(b_ref.dtype)


def ifft_stage_a(y):
    p = y.shape[0]
    w = W_HYENA
    gt = jnp.asarray(_fft_tables()[2], BF16)
    out = pl.pallas_call(
        _ifft_a_kernel,
        out_shape=jax.ShapeDtypeStruct((p, 2, FFT_N1, FFT_N2, w), BF16),
        grid=(p, FFT_N1 // FFT_KB),
        in_specs=[pl.BlockSpec((FFT_KB, 2 * FFT_N2, 2 * FFT_N2), lambda pi, j: (j, 0, 0)),
                  pl.BlockSpec((1, FFT_KB, 2 * FFT_N2, w), lambda pi, j: (pi, j, 0, 0))],
        out_specs=pl.BlockSpec((1, 2, FFT_KB, FFT_N2, w), lambda pi, j: (pi, 0, j, 0, 0)),
        compiler_params=_params(2),
        name="ifft_a",
    )(gt, y)
    return out.reshape(p, 2 * FFT_N1, FFT_N2 * w)


def _ifft_b_kernel(f_ref, b_ref, z_ref, gate_ref, skip_ref, o_ref):
    conv = jnp.dot(f_ref[...], b_ref[0], preferred_element_type=F32)
    o_ref[0] = (gate_ref[0] * (conv + z_ref[0] * skip_ref[...])).astype(o_ref.dtype)


def ifft_stage_b(b, z, gate, skip, out_dtype):
    p, _, cols = b.shape
    f1i = jnp.asarray(_fft_tables()[3], BF16)
    skip_flat = jnp.tile(skip.reshape(1, W_HYENA), (1, FFT_N2))
    return pl.pallas_call(
        _ifft_b_kernel,
        out_shape=jax.ShapeDtypeStruct((p, FFT_N1, cols), out_dtype),
        grid=(p, cols // FFT_TN),
        in_specs=[pl.BlockSpec((FFT_N1, 2 * FFT_N1), lambda pi, j: (0, 0)),
                  pl.BlockSpec((1, 2 * FFT_N1, FFT_TN), lambda pi, j: (pi, 0, j)),
                  pl.BlockSpec((1, FFT_N1, FFT_TN), lambda pi, j: (pi, 0, j)),
                  pl.BlockSpec((1, FFT_N1, FFT_TN), lambda pi, j: (pi, 0, j)),
                  pl.BlockSpec((1, FFT_TN), lambda pi, j: (0, j))],
        out_specs=pl.BlockSpec((1, FFT_N1, FFT_TN), lambda pi, j: (pi, 0, j)),
        compiler_params=_params(2),
        name="ifft_b",
    )(f1i, b, z, gate, skip_flat)


def hyena_long_conv(v, x1, x2, hyena_skip, filt):
    b, length, w = v.shape
    taps = hyena_filter_taps(length, *filt)
    if 2 * length == FFT_N and b % 2 == 0:
        pairs = b // 2
        flat = lambda t: t.reshape(pairs, FFT_N1, FFT_N2 * w)
        hspec = fft_stage2(fft_stage1(taps.reshape(2 * HYENA_ORDER, FFT_N1 // 2, FFT_N2 * w)))
        z = flat(v)
        for o, gate in enumerate((x1, x2)):
            y = fft_stage2(fft_stage1(z), hspec, o)
            z = ifft_stage_b(ifft_stage_a(y), z, flat(gate), hyena_skip[o],
                             F32 if o == 0 else BF16)
        return z.reshape(b, length, w)
    fwd, inv = dft_tables(length)
    hspec = hyena_filter_spectrum(taps, fwd, length)
    y = hyena_forward(v, 0, hspec, 0, fwd, length)
    z = hyena_inverse(y, v, 0, x1, 0, hyena_skip[0], inv, length, F32)
    y = hyena_forward(z, 0, hspec, 1, fwd, length)
    return hyena_inverse(y, z, 0, x2, 0, hyena_skip[1], inv, length, BF16)


def _merge_kernel(ya_ref, zb_ref, oc_ref, ga_ref, gb_ref, gc_ref, bg_ref, wa_ref, wb_ref, wc_ref,
                  wo_ref, x_ref, mg_ref, o_ref):
    d = D_MODEL

    def branch(act_ref, w_ref, g_ref, k):
        br = jnp.dot(act_ref[0], w_ref[...], preferred_element_type=F32)
        return jax.nn.sigmoid(g_ref[0].astype(F32) + bg_ref[:, k * d:(k + 1) * d]) * br

    m = (branch(ya_ref, wa_ref, ga_ref, 0) + branch(zb_ref, wb_ref, gb_ref, 1)
         + branch(oc_ref, wc_ref, gc_ref, 2))
    out = jnp.dot(m.astype(BF16), wo_ref[...], preferred_element_type=F32)
    o_ref[0] = x_ref[0] + mg_ref[0] * out


def gated_merge(ya, zb, oc, y, b_gate, w_a_out, w_b_out, w_c_out, w_o, x, mod_gate):
    b, t, d = x.shape
    tm = min(256, t)
    per_batch = mod_gate.shape[0] > 1
    row = lambda width, cb: pl.BlockSpec((1, tm, width), lambda bi, i: (bi, i, cb))
    const = lambda shape: pl.BlockSpec(shape, lambda bi, i: (0,) * len(shape),
                                       pipeline_mode=pl.Buffered(1))
    g0 = COL_GATE // d
    return pl.pallas_call(
        _merge_kernel,
        out_shape=jax.ShapeDtypeStruct((b, t, d), F32),
        grid=(b, t // tm),
        in_specs=[row(W_CONV, 0), row(W_HYENA, 0), row(W_DIFF, 0),
                  row(d, g0), row(d, g0 + 1), row(d, g0 + 2),
                  const((1, 3 * d)), const((W_CONV, d)), const((W_HYENA, d)), const((W_DIFF, d)),
                  const((d, d)), row(d, 0),
                  pl.BlockSpec((1, 1, d), (lambda bi, i: (bi, 0, 0)) if per_batch
                               else (lambda bi, i: (0, 0, 0)))],
        out_specs=row(d, 0),
        compiler_params=_params(2),
        name="gated_merge",
    )(ya, zb, oc, y, y, y, b_gate.reshape(1, 3 * d), w_a_out.astype(BF16), w_b_out.astype(BF16),
      w_c_out.astype(BF16), w_o.astype(BF16), x, mod_gate)


def _ffn_down_kernel(*refs, tm, tk, final):
    (a_ref, v_ref, ap_ref, vp_ref, an_ref, vn_ref, cw_ref, cb_ref, w_ref, x_ref, mg_ref) = refs[:11]
    o_ref, ext_ref, hid_ref = refs[-3:]
    i = pl.program_id(1)
    last_i = pl.num_programs(1) - 1

    def conv(slot, c_ref, p_ref, n_ref, c0, w0):
        ext_ref[slot, 0:HALO, :] = jnp.where(i == 0, 0.0, p_ref[0, :, c0:c0 + tk].astype(F32))
        ext_ref[slot, HALO:HALO + tm, :] = c_ref[0, :, c0:c0 + tk].astype(F32)
        ext_ref[slot, HALO + tm:2 * HALO + tm, :] = jnp.where(
            i == last_i, 0.0, n_ref[0, :, c0:c0 + tk].astype(F32))
        w = cw_ref[:, w0:w0 + tk]
        return (w[0:1] * ext_ref[slot, HALO - 1:HALO - 1 + tm, :]
                + w[1:2] * ext_ref[slot, HALO:HALO + tm, :]
                + w[2:3] * ext_ref[slot, HALO + 1:HALO + 1 + tm, :] + cb_ref[:, w0:w0 + tk])

    out = None
    for kc in range(D_FF // tk):
        c0 = kc * tk
        a = conv((2 * kc) % 4, a_ref, ap_ref, an_ref, c0, c0)
        v = conv((2 * kc + 1) % 4, v_ref, vp_ref, vn_ref, c0, D_FF + c0)
        hid_ref[kc % 2] = (a * jax.nn.sigmoid(a) * v).astype(BF16)
        part = jnp.dot(hid_ref[kc % 2], w_ref[c0:c0 + tk, :], preferred_element_type=F32)
        out = part if out is None else out + part
    out = x_ref[0] + mg_ref[0] * out
    if final:
        out = _rms(out, refs[11][...])
    o_ref[0] = out


def ffn_down(u, conv_w, conv_b, w_down, layer, x, mod_gate, final_g=None):
    b, t, d = x.shape
    tk = 512
    tm = min(256, t)
    hb = tm // HALO
    n_hb = t // HALO
    per_batch = mod_gate.shape[0] > 1
    main = lambda cb: pl.BlockSpec((1, tm, D_FF), lambda bi, i: (bi, i, cb))
    prev = lambda cb: pl.BlockSpec((1, HALO, D_FF), lambda bi, i: (bi, jnp.maximum(i * hb - 1, 0), cb))
    nxt = lambda cb: pl.BlockSpec((1, HALO, D_FF),
                                  lambda bi, i: (bi, jnp.minimum((i + 1) * hb, n_hb - 1), cb))
    const = lambda shape: pl.BlockSpec(shape, lambda bi, i: (0,) * len(shape))
    in_specs = [main(0), main(1), prev(0), prev(1), nxt(0), nxt(1),
                const((3, 2 * D_FF)), const((1, 2 * D_FF)),
                pl.BlockSpec((None, D_FF, d), lambda bi, i: (layer, 0, 0),
                             pipeline_mode=pl.Buffered(1)),
                pl.BlockSpec((1, tm, d), lambda bi, i: (bi, i, 0)),
                pl.BlockSpec((1, 1, d), (lambda bi, i: (bi, 0, 0)) if per_batch
                             else (lambda bi, i: (0, 0, 0)))]
    args = [u, u, u, u, u, u, conv_w, conv_b.reshape(1, 2 * D_FF), w_down, x, mod_gate]
    if final_g is not None:
        in_specs.append(const((1, d)))
        args.append(final_g.reshape(1, d))
    kern = functools.partial(_ffn_down_kernel, tm=tm, tk=tk, final=final_g is not None)
    return pl.pallas_call(
        kern,
        out_shape=jax.ShapeDtypeStruct((b, t, d), F32),
        grid=(b, t // tm),
        in_specs=in_specs,
        out_specs=pl.BlockSpec((1, tm, d), lambda bi, i: (bi, i, 0)),
        scratch_shapes=[pltpu.VMEM((4, tm + 2 * HALO, tk), F32),
                        pltpu.VMEM((2, tm, tk), BF16)],
        compiler_params=_params(2),
        name="ffn_down",
    )(*args)


def mixer_sublayer(x, y, y_c, y_lat, rope, p, mod_gate, lam_init):
    ya = depthwise_conv(y, COL_A, W_CONV, p['conv_a_w'], p['conv_a_b'], glu=True,
                        ln=(p['ln_a_g'], p['ln_a_b']), out_dtype=BF16, name="conformer_conv")
    v, x1, x2 = [depthwise_conv(y, COL_B + k * W_HYENA, W_HYENA,
                                p['short_b_w'][:, k * W_HYENA:(k + 1) * W_HYENA],
                                p['short_b_b'][k * W_HYENA:(k + 1) * W_HYENA], name="hyena_short_conv")
                 for k in range(3)]
    filt = (p['filt_w1'], p['filt_b1'], p['filt_w2'], p['filt_b2'], p['filt_w3'])
    zb = hyena_long_conv(v, x1, x2, p['hyena_skip'], filt)
    oc = diff_attention(y, y_c, y_lat, rope, p['diff_lambda'], p['subln_g'], lam_init)
    return gated_merge(ya, zb, oc, y, p['b_gate'], p['w_a_out'], p['w_b_out'], p['w_c_out'],
                       p['w_o'], x, mod_gate)


def kernel(x, c, ctx, c_ctx, w_ada, b_ada, norm1_g, norm2_g, w_in, b_gate, conv_a_w, conv_a_b, ln_a_g, ln_a_b, w_a_out, short_b_w, short_b_b, filt_w1, filt_b1, filt_w2, filt_b2, filt_w3, hyena_skip, w_b_out, diff_lambda, subln_g, w_c_out, w_o, w_up, conv_f_w, conv_f_b, w_down, final_g):
    batch = x.shape[0]
    rope = rope_tables(x.shape[1])
    cc = jnp.concatenate([c, c_ctx[None, :], jnp.zeros((8 - batch - 1, D_MODEL), F32)], axis=0)
    gate_rot = N_MIX // TN
    w_in_b, w_up_b, w_down_b = w_in.astype(BF16), w_up.astype(BF16), w_down.astype(BF16)
    for l in range(DEPTH):
        p = {
            'b_gate': b_gate[l],
            'conv_a_w': conv_a_w[l], 'conv_a_b': conv_a_b[l],
            'ln_a_g': ln_a_g[l], 'ln_a_b': ln_a_b[l], 'w_a_out': w_a_out[l],
            'short_b_w': short_b_w[l], 'short_b_b': short_b_b[l],
            'filt_w1': filt_w1[l], 'filt_b1': filt_b1[l], 'filt_w2': filt_w2[l],
            'filt_b2': filt_b2[l], 'filt_w3': filt_w3[l],
            'hyena_skip': hyena_skip[l], 'w_b_out': w_b_out[l],
            'diff_lambda': diff_lambda[l], 'subln_g': subln_g[l], 'w_c_out': w_c_out[l],
            'w_o': w_o[l],
        }
        update_ctx = l < DEPTH - 1
        last = l == DEPTH - 1
        lam_init = 0.8 - 0.6 * math.exp(-0.3 * l)
        mod = ada_projection(cc, w_ada, l, b_ada[l])
        mx = [mod[:batch, None, j * D_MODEL:(j + 1) * D_MODEL] for j in range(6)]
        mc = [mod[batch:batch + 1, None, j * D_MODEL:(j + 1) * D_MODEL] for j in range(6)]

        y_x = norm_mod_matmul(x, norm1_g[l], mx[0], mx[1], w_in_b, l, gate_rot, "mm_in_x")
        t_c = ctx.shape[1]
        merged = lambda a: a.reshape(1, batch * t_c, a.shape[2])
        split = lambda a: a.reshape(batch, t_c, a.shape[2])
        y_c = split(norm_mod_matmul(merged(ctx), norm1_g[l], mc[0], mc[1], w_in_b, l, gate_rot,
                                    "mm_in_c"))
        x = mixer_sublayer(x, y_x, y_c, y_x, rope, p, mx[2], lam_init)
        u = norm_mod_matmul(x, norm2_g[l], mx[3], mx[4], w_up_b, l, 0, "mm_up_x")
        x = ffn_down(u, conv_f_w[l], conv_f_b[l], w_down_b, l, x, mx[5], final_g if last else None)
        if update_ctx:
            ctx = mixer_sublayer(ctx, y_c, y_c, None, None, p, mc[2], lam_init)
            u = split(norm_mod_matmul(merged(ctx), norm2_g[l], mc[3], mc[4], w_up_b, l, 0, "mm_up_c"))
            ctx = ffn_down(u, conv_f_w[l], conv_f_b[l], w_down_b, l, ctx, mc[5])
    return x
```

```python
import functools
import math

import numpy as np
import jax
import jax.numpy as jnp
from jax import lax
from jax.experimental import pallas as pl
from jax.experimental.pallas import tpu as pltpu

D_MODEL = 2048
DEPTH = 2
GRID_W = 64
W_CONV = D_MODEL // 4
W_HYENA = D_MODEL // 4
N_HEADS = 8
HEAD_DIM = 64
HEAD_W = 2 * HEAD_DIM
W_DIFF = N_HEADS * HEAD_W
D_FF = 5632
HYENA_ORDER = 2
HYENA_BANDS = 8
HYENA_TARGET = 1e-2
HYENA_FAST_PCT = 0.3
HYENA_SLOW_PCT = 1.5
ROPE_BASE = 10000.0
EPS = 1e-6
LOG2E = 1.4426950408889634

LANE = 128
SUBLANES = 8
HALO = 16
FREQ_BLOCK = 256
TN = 512
VMEM_LIMIT_BYTES = 56 * 1024 * 1024

N_IN = 2 * W_CONV + 3 * W_HYENA + 3 * W_DIFF + 3 * D_MODEL
N_MIX = N_IN - 3 * D_MODEL
COL_GATE = 0
COL_A = 3 * D_MODEL
COL_B = COL_A + 2 * W_CONV
COL_Q = COL_B + 3 * W_HYENA
COL_K = COL_Q + W_DIFF
COL_V = COL_K + W_DIFF

F32 = jnp.float32
BF16 = jnp.bfloat16


def _params(n_axes):
    return pltpu.CompilerParams(
        dimension_semantics=("arbitrary",) * n_axes, vmem_limit_bytes=VMEM_LIMIT_BYTES)


def _rms(x, g):
    return x * lax.rsqrt(jnp.mean(x * x, axis=-1, keepdims=True) + EPS) * g


def _ada_kernel(c_ref, w_ref, b_ref, o_ref):
    c = c_ref[...]
    h = c * jax.nn.sigmoid(c)
    o_ref[...] = jnp.dot(h, w_ref[...], precision=lax.Precision.HIGHEST,
                         preferred_element_type=F32) + b_ref[...]


def ada_projection(cc, w, layer, b):
    m, d = cc.shape
    n = w.shape[2]
    tn = 1024
    return pl.pallas_call(
        _ada_kernel,
        out_shape=jax.ShapeDtypeStruct((m, n), F32),
        grid=(n // tn,),
        in_specs=[pl.BlockSpec((m, d), lambda j: (0, 0)),
                  pl.BlockSpec((None, d, tn), lambda j: (layer, 0, j)),
                  pl.BlockSpec((1, tn), lambda j: (0, j))],
        out_specs=pl.BlockSpec((m, tn), lambda j: (0, j)),
        compiler_params=_params(1),
        name="ada_proj",
    )(cc, w, b.reshape(1, n))


def _nmm_kernel(x_ref, g_ref, sh_ref, sc_ref, w_ref, o_ref, h_ref):
    @pl.when(pl.program_id(2) == 0)
    def _():
        rb = min(256, h_ref.shape[0])

        def rows(r, carry):
            r0 = pl.multiple_of(r * rb, rb)
            y = _rms(x_ref[0, pl.ds(r0, rb), :], g_ref[...])
            h_ref[pl.ds(r0, rb), :] = (y * (1.0 + sc_ref[0]) + sh_ref[0]).astype(BF16)
            return carry

        lax.fori_loop(0, h_ref.shape[0] // rb, rows, 0)

    o_ref[0] = jnp.dot(h_ref[...], w_ref[...], preferred_element_type=F32).astype(o_ref.dtype)


def norm_mod_matmul(x, g, shift, scale, w, layer, col_rot, name):
    b, t, d = x.shape
    n = w.shape[2]
    tm = min(2048, t)
    nt = n // TN
    per_batch = shift.shape[0] > 1
    mod_spec = pl.BlockSpec((1, 1, d), (lambda bi, i, j: (bi, 0, 0)) if per_batch
                            else (lambda bi, i, j: (0, 0, 0)))
    return pl.pallas_call(
        _nmm_kernel,
        out_shape=jax.ShapeDtypeStruct((b, t, n), BF16),
        grid=(b, t // tm, nt),
        in_specs=[pl.BlockSpec((1, tm, d), lambda bi, i, j: (bi, i, 0)),
                  pl.BlockSpec((1, d), lambda bi, i, j: (0, 0)),
                  mod_spec, mod_spec,
                  pl.BlockSpec((None, d, TN), lambda bi, i, j: (layer, 0, (j + col_rot) % nt))],
        out_specs=pl.BlockSpec((1, tm, TN), lambda bi, i, j: (bi, i, j)),
        scratch_shapes=[pltpu.VMEM((tm, d), BF16)],
        compiler_params=_params(3),
        name=name,
    )(x, g.reshape(1, d), shift, scale, w)


def _dwconv_kernel(*refs, taps, glu, post, tt, rb):
    n_in = 2 if glu else 1
    main = refs[:n_in]
    prev = refs[n_in:2 * n_in]
    nxt = refs[2 * n_in:3 * n_in]
    rest = refs[3 * n_in:]
    shifted = taps > SUBLANES
    if shifted:
        rest, sh_ref = rest[:-1], rest[-1]
    if post:
        w_ref, b_ref, lg_ref, lb_ref, o_ref, ext_ref = rest
    else:
        w_ref, b_ref, o_ref, ext_ref = rest
    i = pl.program_id(1)
    last = pl.num_programs(1) - 1

    def value(blocks):
        a = blocks[0][0].astype(F32)
        if glu:
            a = a * jax.nn.sigmoid(blocks[1][0].astype(F32))
        return a

    ext_ref[0:HALO, :] = jnp.where(i == 0, 0.0, value(prev))
    ext_ref[HALO:HALO + tt, :] = value(main)
    ext_ref[HALO + tt:2 * HALO + tt, :] = jnp.where(i == last, 0.0, value(nxt))

    pad = (taps - 1) // 2
    if shifted:
        span = sh_ref.shape[1]
        for s in range(1, SUBLANES):
            sh_ref[s - 1] = ext_ref[s:s + span, :]

    def window(start):
        s = start % SUBLANES
        if not shifted or s == 0:
            return ext_ref[start:start + rb, :]
        return sh_ref[s - 1, start - s:start - s + rb, :]

    for r in range(tt // rb):
        acc = jnp.zeros((rb, ext_ref.shape[1]), F32) + b_ref[...]
        for j in range(taps):
            acc = acc + w_ref[j:j + 1, :] * window(HALO - pad + j + r * rb)
        if post:
            mu = jnp.mean(acc, axis=-1, keepdims=True)
            cen = acc - mu
            var = jnp.mean(cen * cen, axis=-1, keepdims=True)
            y = cen * lax.rsqrt(var + EPS) * lg_ref[...] + lb_ref[...]
            acc = y * jax.nn.sigmoid(y)
        o_ref[0, r * rb:(r + 1) * rb, :] = acc.astype(o_ref.dtype)


def depthwise_conv(y, col, width, w, bias, glu=False, ln=None, out_dtype=F32, name="dwconv"):
    b, t, _ = y.shape
    taps = w.shape[0]
    tt = min(512, t)
    cw = width if ln is not None else min(512, width)
    ncb = width // cw
    cb0 = col // cw
    hb = tt // HALO
    n_hb = t // HALO
    cols = [cb0, cb0 + ncb] if glu else [cb0]

    def main_spec(c0):
        return pl.BlockSpec((1, tt, cw), lambda bi, i, c: (bi, i, c0 + c))

    def prev_spec(c0):
        return pl.BlockSpec((1, HALO, cw), lambda bi, i, c: (bi, jnp.maximum(i * hb - 1, 0), c0 + c))

    def next_spec(c0):
        return pl.BlockSpec((1, HALO, cw),
                            lambda bi, i, c: (bi, jnp.minimum((i + 1) * hb, n_hb - 1), c0 + c))

    vec = lambda rows: pl.BlockSpec((rows, cw), lambda bi, i, c: (0, c))
    in_specs = ([main_spec(c0) for c0 in cols] + [prev_spec(c0) for c0 in cols]
                + [next_spec(c0) for c0 in cols] + [vec(taps), vec(1)])
    args = [y] * (3 * len(cols)) + [w, bias.reshape(1, width)]
    if ln is not None:
        in_specs += [vec(1), vec(1)]
        args += [ln[0].reshape(1, width), ln[1].reshape(1, width)]
    kern = functools.partial(_dwconv_kernel, taps=taps, glu=glu, post=ln is not None, tt=tt,
                             rb=min(128, tt))
    return pl.pallas_call(
        kern,
        out_shape=jax.ShapeDtypeStruct((b, t, width), out_dtype),
        grid=(b, t // tt, ncb),
        in_specs=in_specs,
        out_specs=pl.BlockSpec((1, tt, cw), lambda bi, i, c: (bi, i, c)),
        scratch_shapes=[pltpu.VMEM((tt + 2 * HALO, cw), F32)]
        + ([pltpu.VMEM((SUBLANES - 1, tt + 2 * HALO - SUBLANES, cw), F32)] if taps > SUBLANES else []),
        compiler_params=_params(3),
        name=name,
    )(*args)


def _swap_halves(x):
    lanes = lax.broadcasted_iota(jnp.int32, x.shape, 1)
    first = (lanes % HEAD_DIM) < (HEAD_DIM // 2)
    return jnp.where(first, pltpu.roll(x, LANE - HEAD_DIM // 2, 1), pltpu.roll(x, HEAD_DIM // 2, 1))


def _attn_kernel(*refs, lam_init, n_sub, t_c, has_lat):
    if has_lat:
        (q_ref, kc_ref, vc_ref, kx_ref, vx_ref, cq_ref, sq_ref, ck_ref, sk_ref, dl_ref, g_ref,
         o_ref, s_ref, p_ref, k_s, vt_s) = refs
    else:
        q_ref, kc_ref, vc_ref, dl_ref, g_ref, o_ref, s_ref, p_ref, k_s, vt_s = refs
    t_k = k_s.shape[0]
    tq = q_ref.shape[1]
    ts = tq // n_sub
    ck = 256

    @pl.when(pl.program_id(2) == 0)
    def _():
        k_s[0:t_c, :] = kc_ref[0]
        if has_lat:
            kx = kx_ref[0].astype(F32)
            k_s[t_c:, :] = (kx * ck_ref[...] + _swap_halves(kx) * sk_ref[...]).astype(BF16)
        for c in range(t_k // ck):
            lo = c * ck
            src = vc_ref[0, lo:lo + ck, :] if lo < t_c else vx_ref[0, lo - t_c:lo - t_c + ck, :]
            vt_s[:, lo:lo + ck] = src.astype(F32).T.astype(BF16)

    q = q_ref[0].astype(F32)
    if has_lat:
        q = q * cq_ref[...] + _swap_halves(q) * sq_ref[...]
    qt = (q * (HEAD_DIM ** -0.5 * LOG2E)).T
    rows = lax.broadcasted_iota(jnp.int32, qt.shape, 0)
    qts = (jnp.where(rows < HEAD_DIM, qt, 0.0).astype(BF16),
           jnp.where(rows >= HEAD_DIM, qt, 0.0).astype(BF16))

    kk = k_s[...]
    vt = vt_s[...]
    streams = [(mi, si) for si in range(n_sub) for mi in range(2)]
    ms = {}
    for mi, si in streams:
        s = jnp.dot(kk, qts[mi][:, si * ts:(si + 1) * ts], preferred_element_type=F32)
        s_ref[mi, si] = s
        ms[mi, si] = jnp.max(s, axis=0, keepdims=True)
    outs = {}
    for mi, si in streams:
        p = jnp.exp2(s_ref[mi, si] - ms[mi, si])
        l = jnp.sum(p, axis=0, keepdims=True)
        p_ref[mi, si] = p.astype(BF16)
        acc = jnp.dot(vt, p_ref[mi, si], preferred_element_type=F32)
        outs[mi, si] = acc / l

    dl = dl_ref[...]
    s1 = jnp.sum(dl[0:1] * dl[1:2], axis=-1, keepdims=True)
    s2 = jnp.sum(dl[2:3] * dl[3:4], axis=-1, keepdims=True)
    lam = jnp.exp(s1) - jnp.exp(s2) + lam_init
    for si in range(n_sub):
        ot = outs[0, si] - lam * outs[1, si]
        ot = ot * lax.rsqrt(jnp.mean(ot * ot, axis=0, keepdims=True) + EPS)
        o = ot.T
        o_ref[0, si * ts:(si + 1) * ts, :] = (o * g_ref[0] * (1.0 - lam_init)).astype(o_ref.dtype)


def diff_attention(y_q, y_c, y_x, rope, diff_lambda, subln_g, lam_init):
    b, t_q, _ = y_q.shape
    t_c = y_c.shape[1]
    has_lat = y_x is not None
    t_k = t_c + (y_x.shape[1] if has_lat else 0)
    tq = min(512, t_q)
    ts = min(256, tq)
    n_sub = tq // ts
    qb, kb, vb = COL_Q // HEAD_W, COL_K // HEAD_W, COL_V // HEAD_W
    head = lambda base, rows: pl.BlockSpec((1, rows, HEAD_W), lambda bi, hi, i: (bi, 0, base + hi))
    in_specs = [pl.BlockSpec((1, tq, HEAD_W), lambda bi, hi, i: (bi, i, qb + hi)),
                head(kb, t_c), head(vb, t_c)]
    args = [y_q, y_c, y_c]
    if has_lat:
        t_x = y_x.shape[1]
        in_specs += [head(kb, t_x), head(vb, t_x),
                     pl.BlockSpec((tq, HEAD_W), lambda bi, hi, i: (i, 0)),
                     pl.BlockSpec((tq, HEAD_W), lambda bi, hi, i: (i, 0)),
                     pl.BlockSpec((t_x, HEAD_W), lambda bi, hi, i: (0, 0)),
                     pl.BlockSpec((t_x, HEAD_W), lambda bi, hi, i: (0, 0))]
        args += [y_x, y_x, rope[0], rope[1], rope[0], rope[1]]
    in_specs += [pl.BlockSpec((4, HEAD_DIM), lambda bi, hi, i: (0, 0)),
                 pl.BlockSpec((1, 1, HEAD_W), lambda bi, hi, i: (hi, 0, 0))]
    args += [diff_lambda.astype(F32), subln_g.astype(F32).reshape(N_HEADS, 1, HEAD_W)]
    kern = functools.partial(_attn_kernel, lam_init=lam_init, n_sub=n_sub, t_c=t_c, has_lat=has_lat)
    return pl.pallas_call(
        kern,
        out_shape=jax.ShapeDtypeStruct((b, t_q, W_DIFF), BF16),
        grid=(b, N_HEADS, t_q // tq),
        in_specs=in_specs,
        out_specs=pl.BlockSpec((1, tq, HEAD_W), lambda bi, hi, i: (bi, i, hi)),
        scratch_shapes=[pltpu.VMEM((2, n_sub, t_k, ts), F32),
                        pltpu.VMEM((2, n_sub, t_k, ts), BF16),
                        pltpu.VMEM((t_k, HEAD_W), BF16),
                        pltpu.VMEM((HEAD_W, t_k), BF16)],
        compiler_params=_params(3),
        name="diff_attn",
    )(*args)


def rope_tables(t):
    rows = t // GRID_W
    row = np.repeat(np.arange(rows), GRID_W).astype(np.float64)
    col = np.tile(np.arange(GRID_W), rows).astype(np.float64)
    n_freq = HEAD_DIM // 4
    inv = (ROPE_BASE ** (-np.arange(n_freq, dtype=np.float32) / n_freq)).astype(np.float64)
    ang = np.concatenate([row[:, None] * inv, col[:, None] * inv], axis=-1)
    c, s = np.cos(ang), np.sin(ang)
    cos_t = np.tile(np.concatenate([c, c], axis=-1), (1, 2))
    sin_t = np.tile(np.concatenate([-s, s], axis=-1), (1, 2))
    return jnp.asarray(cos_t, F32), jnp.asarray(sin_t, F32)


@functools.lru_cache(maxsize=None)
def _dft_factors(length):
    n = 2 * length
    fb = min(FREQ_BLOCK, length)
    nb = length // fb
    kk = np.arange(length, dtype=np.int64).reshape(nb, 1, fb)
    k_rows = np.concatenate([kk, kk], axis=1)
    is_sin = np.zeros((nb, 2, fb), bool)
    is_sin[:, 1, :] = True
    k_rows[0, 1, 0] = length
    is_sin[0, 1, 0] = False
    k_rows = k_rows.reshape(-1)
    is_sin = is_sin.reshape(-1)
    n_hi = max(length // LANE, 1)
    n_lo = min(LANE, length)
    a_lo = 2.0 * np.pi * ((k_rows[:, None] * np.arange(n_lo)[None, :]) % n) / n
    a_hi = 2.0 * np.pi * ((k_rows[:, None] * (LANE * np.arange(n_hi))[None, :]) % n) / n
    c_lo, s_lo, c_hi, s_hi = np.cos(a_lo), np.sin(a_lo), np.cos(a_hi), np.sin(a_hi)
    p = np.where(is_sin[:, None], s_hi, c_hi)
    q = np.where(is_sin[:, None], c_hi, -s_hi)
    scale = np.full((2 * length,), 2.0 / n)
    scale[0] = 1.0 / n
    scale[fb] = 1.0 / n
    f32 = lambda a: np.asarray(a, np.float32)
    return f32(p), f32(q), f32(c_lo), f32(s_lo), f32(scale.reshape(-1, 1))


def dft_tables(length):
    p, q, c_lo, s_lo, _ = _dft_factors(length)
    fwd = (jnp.asarray(p)[:, :, None] * jnp.asarray(c_lo)[:, None, :]
           + jnp.asarray(q)[:, :, None] * jnp.asarray(s_lo)[:, None, :])
    fwd = fwd.reshape(2 * length, length).astype(BF16)
    return fwd, fwd.T


def _cmul(za, zb, ha, hb, first_rows):
    ya = za * ha - zb * hb
    yb = za * hb + zb * ha
    ya = jnp.where(first_rows, za * ha, ya)
    yb = jnp.where(first_rows, zb * hb, yb)
    return ya, yb


def _fwd_kernel(f_ref, z_ref, h_ref, c_ref, y_ref, zb_ref, *, fb):
    @pl.when(pl.program_id(1) == 0)
    def _():
        zb_ref[...] = z_ref[0].astype(BF16)

    spec = jnp.dot(f_ref[...], zb_ref[...], preferred_element_type=F32)
    za, zb = spec[:fb], spec[fb:]
    h = h_ref[...]
    rows = lax.broadcasted_iota(jnp.int32, za.shape, 0)
    first = jnp.logical_and(rows == 0, pl.program_id(1) == 0)
    ya, yb = _cmul(za, zb, h[:fb], h[fb:], first)
    c = c_ref[...]
    y_ref[0, :fb, :] = (ya * c[:fb]).astype(y_ref.dtype)
    y_ref[0, fb:, :] = (yb * c[fb:]).astype(y_ref.dtype)


def hyena_forward(z, z_col, hspec, order, fwd, length):
    b = z.shape[0]
    w = W_HYENA
    scale = jnp.asarray(_dft_factors(length)[4])
    fb = min(FREQ_BLOCK, length)
    kern = functools.partial(_fwd_kernel, fb=fb)
    return pl.pallas_call(
        kern,
        out_shape=jax.ShapeDtypeStruct((b, 2 * length, w), BF16),
        grid=(b, length // fb),
        in_specs=[pl.BlockSpec((2 * fb, length), lambda bi, i: (i, 0)),
                  pl.BlockSpec((1, length, w), lambda bi, i: (bi, 0, z_col)),
                  pl.BlockSpec((2 * fb, w), lambda bi, i: (i, order)),
                  pl.BlockSpec((2 * fb, 1), lambda bi, i: (i, 0))],
        out_specs=pl.BlockSpec((1, 2 * fb, w), lambda bi, i: (bi, i, 0)),
        scratch_shapes=[pltpu.VMEM((length, w), BF16)],
        compiler_params=_params(2),
        name="hyena_fwd",
    )(fwd, z, hspec, scale)


def _inv_kernel(g_ref, y_ref, z_ref, gate_ref, skip_ref, o_ref):
    conv = jnp.dot(g_ref[...], y_ref[0], preferred_element_type=F32)
    o_ref[0] = (gate_ref[0] * (conv + z_ref[0] * skip_ref[...])).astype(o_ref.dtype)


def hyena_inverse(y, z, z_col, gate, gate_col, skip, inv, length, out_dtype):
    b = z.shape[0]
    w = W_HYENA
    tm = min(512, length)
    return pl.pallas_call(
        _inv_kernel,
        out_shape=jax.ShapeDtypeStruct((b, length, w), out_dtype),
        grid=(b, length // tm),
        in_specs=[pl.BlockSpec((tm, 2 * length), lambda bi, i: (i, 0)),
                  pl.BlockSpec((1, 2 * length, w), lambda bi, i: (bi, 0, 0)),
                  pl.BlockSpec((1, tm, w), lambda bi, i: (bi, i, z_col)),
                  pl.BlockSpec((1, tm, w), lambda bi, i: (bi, i, gate_col)),
                  pl.BlockSpec((1, w), lambda bi, i: (0, 0))],
        out_specs=pl.BlockSpec((1, tm, w), lambda bi, i: (bi, i, 0)),
        compiler_params=_params(2),
        name="hyena_inv",
    )(inv, y, z, gate, skip.reshape(1, w))


def _filt_kernel(feat_ref, w1_ref, b1_ref, w2_ref, b2_ref, w3f_ref, w3b_ref, dl_ref,
                 taps_ref, hf_ref, hb_ref, *, rb):
    hi = lax.Precision.HIGHEST
    length, ct = hf_ref.shape
    n_blocks = length // rb

    def taps(i, norm):
        r0 = pl.multiple_of(i * rb, rb)
        feats = feat_ref[pl.ds(r0, rb), :]
        h = jnp.sin(jnp.dot(feats, w1_ref[...], precision=hi, preferred_element_type=F32)
                    + b1_ref[...])
        h = jnp.sin(jnp.dot(h, w2_ref[...], precision=hi, preferred_element_type=F32)
                    + b2_ref[...])
        decay = jnp.exp(-feats[:, 0:1] * dl_ref[...])
        hf = jnp.dot(h, w3f_ref[...], precision=hi, preferred_element_type=F32) * decay
        hb = jnp.dot(h, w3b_ref[...], precision=hi, preferred_element_type=F32) * decay
        rows = lax.broadcasted_iota(jnp.int32, hb.shape, 0) + r0
        hb = jnp.where(rows == 0, 0.0, hb)
        hf_ref[pl.ds(r0, rb), :] = hf
        hb_ref[pl.ds(r0, rb), :] = hb
        return (norm + jnp.sum(jnp.abs(hf), axis=0, keepdims=True)
                + jnp.sum(jnp.abs(hb), axis=0, keepdims=True))

    norm = lax.fori_loop(0, n_blocks, taps, jnp.zeros((1, ct), F32))
    inv = 1.0 / norm

    def normalise(i, carry):
        r0 = pl.multiple_of(i * rb, rb)
        taps_ref[0, pl.ds(r0, rb), :] = hf_ref[pl.ds(r0, rb), :] * inv
        taps_ref[1, pl.ds(r0, rb), :] = hb_ref[pl.ds(r0, rb), :] * inv
        return carry

    lax.fori_loop(0, n_blocks, normalise, 0)


def hyena_filter_taps(length, filt_w1, filt_b1, filt_w2, filt_b2, filt_w3):
    w = W_HYENA
    ct = 256
    pos = np.arange(length, dtype=np.float64)
    t = pos / max(length - 1, 1)
    bands = np.linspace(1e-4, HYENA_BANDS - 1, HYENA_BANDS)
    ang = (2.0 * math.pi * pos / length)[:, None] * bands[None, :]
    feats = np.concatenate([t[:, None], np.cos(ang), -np.sin(ang)], axis=-1).astype(np.float32)
    deltas = np.abs(np.linspace(math.log(HYENA_TARGET) / HYENA_SLOW_PCT,
                                math.log(HYENA_TARGET) / HYENA_FAST_PCT, w)).astype(np.float32)
    emb, nf = filt_w1.shape
    npc = w // ct
    full = lambda shape: pl.BlockSpec(shape, lambda o, c: (0,) * len(shape))
    kern = functools.partial(_filt_kernel, rb=min(256, length))
    return pl.pallas_call(
        kern,
        out_shape=jax.ShapeDtypeStruct((HYENA_ORDER, 2, length, w), F32),
        grid=(HYENA_ORDER, npc),
        in_specs=[full((length, emb)), full((emb, nf)), full((1, nf)), full((nf, nf)), full((1, nf)),
                  pl.BlockSpec((nf, ct), lambda o, c: (0, o * 2 * npc + c)),
                  pl.BlockSpec((nf, ct), lambda o, c: (0, o * 2 * npc + npc + c)),
                  pl.BlockSpec((1, ct), lambda o, c: (0, c))],
        out_specs=pl.BlockSpec((None, 2, length, ct), lambda o, c: (o, 0, 0, c)),
        scratch_shapes=[pltpu.VMEM((length, ct), F32),
                        pltpu.VMEM((length, ct), F32)],
        compiler_params=_params(2),
        name="hyena_filt",
    )(jnp.asarray(feats), filt_w1, filt_b1.reshape(1, nf), filt_w2, filt_b2.reshape(1, nf),
      filt_w3, filt_w3, jnp.asarray(deltas).reshape(1, w))


def _hspec_kernel(f_ref, t_ref, o_ref, *, fb):
    f = f_ref[...]
    hs = (t_ref[0] + t_ref[1]).astype(BF16)
    hd = (t_ref[0] - t_ref[1]).astype(BF16)
    a = jnp.dot(f, hs, preferred_element_type=F32)
    bd = jnp.dot(f[fb:], hd, preferred_element_type=F32)
    rows = lax.broadcasted_iota(jnp.int32, bd.shape, 0)
    first = jnp.logical_and(rows == 0, pl.program_id(1) == 0)
    o_ref[:fb, :] = a[:fb]
    o_ref[fb:, :] = jnp.where(first, a[fb:], bd)


def hyena_filter_spectrum(taps, fwd, length):
    fb = min(FREQ_BLOCK, length)
    w = W_HYENA
    kern = functools.partial(_hspec_kernel, fb=fb)
    return pl.pallas_call(
        kern,
        out_shape=jax.ShapeDtypeStruct((2 * length, HYENA_ORDER * w), F32),
        grid=(HYENA_ORDER, length // fb),
        in_specs=[pl.BlockSpec((2 * fb, length), lambda j, i: (i, 0)),
                  pl.BlockSpec((None, 2, length, w), lambda j, i: (j, 0, 0, 0))],
        out_specs=pl.BlockSpec((2 * fb, w), lambda j, i: (i, j)),
        compiler_params=_params(2),
        name="hyena_hspec",
    )(fwd, taps)


FFT_N1 = 64
FFT_N2 = 128
FFT_N = FFT_N1 * FFT_N2
FFT_KB = 8
FFT_TN = 8192


@functools.lru_cache(maxsize=None)
def _fft_tables():
    n1h = FFT_N1 // 2
    k1 = np.arange(FFT_N1)
    ang = 2.0 * np.pi * np.outer(k1, np.arange(n1h)) / FFT_N1
    c, s = np.cos(ang), np.sin(ang)
    f1 = np.block([[c, s], [-s, c]])
    f1i = np.block([[c.T, -s.T], [s.T, c.T]]) / FFT_N
    k2 = np.arange(FFT_N2)
    n2 = np.arange(FFT_N2)
    kfull = k1[:, None, None] + FFT_N1 * k2[None, :, None]
    ang2 = 2.0 * np.pi * ((kfull * n2[None, None, :]) % FFT_N) / FFT_N
    gc, gs = np.cos(ang2), np.sin(ang2)
    g = np.concatenate([np.concatenate([gc, gs], axis=2),
                        np.concatenate([-gs, gc], axis=2)], axis=1)
    f32 = lambda a: np.asarray(a, np.float32)
    return f32(f1), f32(g), f32(np.transpose(g, (0, 2, 1))), f32(f1i)


def _fft_s1_kernel(f_ref, x_ref, o_ref):
    o_ref[0] = jnp.dot(f_ref[...], x_ref[0].astype(BF16),
                       preferred_element_type=F32).astype(o_ref.dtype)


def fft_stage1(x):
    p, rows, cols = x.shape
    f1 = jnp.asarray(_fft_tables()[0][:, :rows], BF16)
    return pl.pallas_call(
        _fft_s1_kernel,
        out_shape=jax.ShapeDtypeStruct((p, 2 * FFT_N1, cols), BF16),
        grid=(p, cols // FFT_TN),
        in_specs=[pl.BlockSpec((2 * FFT_N1, rows), lambda pi, j: (0, 0)),
                  pl.BlockSpec((1, rows, FFT_TN), lambda pi, j: (pi, 0, j))],
        out_specs=pl.BlockSpec((1, 2 * FFT_N1, FFT_TN), lambda pi, j: (pi, 0, j)),
        compiler_params=_params(2),
        name="fft_s1",
    )(f1, x)


def _fft_s2_kernel(*refs, with_filter):
    if with_filter:
        g_ref, a_ref, hp_ref, hq_ref, y_ref = refs
    else:
        g_ref, a_ref, y_ref = refs
    half = FFT_N2
    for j in range(FFT_KB):
        rhs = jnp.concatenate([a_ref[0, 0, j], a_ref[0, 1, j]], axis=0)
        s = jnp.dot(g_ref[j], rhs, preferred_element_type=F32)
        if with_filter:
            hre = hp_ref[0, j, :half, :] + hq_ref[0, j, :half, :]
            him = hp_ref[0, j, half:, :] - hq_ref[0, j, half:, :]
            sre, sim = s[:half], s[half:]
            y_ref[0, j, :half, :] = (sre * hre - sim * him).astype(y_ref.dtype)
            y_ref[0, j, half:, :] = (sre * him + sim * hre).astype(y_ref.dtype)
        else:
            y_ref[0, j] = s.astype(y_ref.dtype)


def fft_stage2(a, hspec=None, order=0):
    p = a.shape[0]
    w = W_HYENA
    g = jnp.asarray(_fft_tables()[1], BF16)
    a5 = a.reshape(p, 2, FFT_N1, FFT_N2, w)
    with_filter = hspec is not None
    in_specs = [pl.BlockSpec((FFT_KB, 2 * FFT_N2, 2 * FFT_N2), lambda pi, j: (j, 0, 0)),
                pl.BlockSpec((1, 2, FFT_KB, FFT_N2, w), lambda pi, j: (pi, 0, j, 0, 0))]
    args = [g, a5]
    if with_filter:
        in_specs += [pl.BlockSpec((1, FFT_KB, 2 * FFT_N2, w), lambda pi, j: (2 * order, j, 0, 0)),
                     pl.BlockSpec((1, FFT_KB, 2 * FFT_N2, w), lambda pi, j: (2 * order + 1, j, 0, 0))]
        args += [hspec, hspec]
    return pl.pallas_call(
        functools.partial(_fft_s2_kernel, with_filter=with_filter),
        out_shape=jax.ShapeDtypeStruct((p, FFT_N1, 2 * FFT_N2, w), BF16 if with_filter else F32),
        grid=(p, FFT_N1 // FFT_KB),
        in_specs=in_specs,
        out_specs=pl.BlockSpec((1, FFT_KB, 2 * FFT_N2, w), lambda pi, j: (pi, j, 0, 0)),
        compiler_params=_params(2),
        name="fft_s2",
    )(*args)


def _ifft_a_kernel(gt_ref, y_ref, b_ref):
    half = FFT_N2
    for j in range(FFT_KB):
        b = jnp.dot(gt_ref[j], y_ref[0, j], preferred_element_type=F32)
        b_ref[0, 0, j] = b[:half].astype(b_ref.dtype)
        b_ref[0, 1, j] = b[half:].astype(b_ref.dtype)


def ifft_stage_a(y):
    p = y.shape[0]
    w = W_HYENA
    gt = jnp.asarray(_fft_tables()[2], BF16)
    out = pl.pallas_call(
        _ifft_a_kernel,
        out_shape=jax.ShapeDtypeStruct((p, 2, FFT_N1, FFT_N2, w), BF16),
        grid=(p, FFT_N1 // FFT_KB),
        in_specs=[pl.BlockSpec((FFT_KB, 2 * FFT_N2, 2 * FFT_N2), lambda pi, j: (j, 0, 0)),
                  pl.BlockSpec((1, FFT_KB, 2 * FFT_N2, w), lambda pi, j: (pi, j, 0, 0))],
        out_specs=pl.BlockSpec((1, 2, FFT_KB, FFT_N2, w), lambda pi, j: (pi, 0, j, 0, 0)),
        compiler_params=_params(2),
        name="ifft_a",
    )(gt, y)
    return out.reshape(p, 2 * FFT_N1, FFT_N2 * w)


def _ifft_b_kernel(f_ref, b_ref, z_ref, gate_ref, skip_ref, o_ref):
    conv = jnp.dot(f_ref[...], b_ref[0], preferred_element_type=F32)
    o_ref[0] = (gate_ref[0] * (conv + z_ref[0] * skip_ref[...])).astype(o_ref.dtype)


def ifft_stage_b(b, z, gate, skip, out_dtype):
    p, _, cols = b.shape
    f1i = jnp.asarray(_fft_tables()[3], BF16)
    skip_flat = jnp.tile(skip.reshape(1, W_HYENA), (1, FFT_N2))
    return pl.pallas_call(
        _ifft_b_kernel,
        out_shape=jax.ShapeDtypeStruct((p, FFT_N1, cols), out_dtype),
        grid=(p, cols // FFT_TN),
        in_specs=[pl.BlockSpec((FFT_N1, 2 * FFT_N1), lambda pi, j: (0, 0)),
                  pl.BlockSpec((1, 2 * FFT_N1, FFT_TN), lambda pi, j: (pi, 0, j)),
                  pl.BlockSpec((1, FFT_N1, FFT_TN), lambda pi, j: (pi, 0, j)),
                  pl.BlockSpec((1, FFT_N1, FFT_TN), lambda pi, j: (pi, 0, j)),
                  pl.BlockSpec((1, FFT_TN), lambda pi, j: (0, j))],
        out_specs=pl.BlockSpec((1, FFT_N1, FFT_TN), lambda pi, j: (pi, 0, j)),
        compiler_params=_params(2),
        name="ifft_b",
    )(f1i, b, z, gate, skip_flat)


def hyena_long_conv(v, x1, x2, hyena_skip, filt):
    b, length, w = v.shape
    taps = hyena_filter_taps(length, *filt)
    if 2 * length == FFT_N and b % 2 == 0:
        pairs = b // 2
        flat = lambda t: t.reshape(pairs, FFT_N1, FFT_N2 * w)
        hspec = fft_stage2(fft_stage1(taps.reshape(2 * HYENA_ORDER, FFT_N1 // 2, FFT_N2 * w)))
        z = flat(v)
        for o, gate in enumerate((x1, x2)):
            y = fft_stage2(fft_stage1(z), hspec, o)
            z = ifft_stage_b(ifft_stage_a(y), z, flat(gate), hyena_skip[o],
                             F32 if o == 0 else BF16)
        return z.reshape(b, length, w)
    fwd, inv = dft_tables(length)
    hspec = hyena_filter_spectrum(taps, fwd, length)
    y = hyena_forward(v, 0, hspec, 0, fwd, length)
    z = hyena_inverse(y, v, 0, x1, 0, hyena_skip[0], inv, length, F32)
    y = hyena_forward(z, 0, hspec, 1, fwd, length)
    return hyena_inverse(y, z, 0, x2, 0, hyena_skip[1], inv, length, BF16)


def _merge_kernel(ya_ref, zb_ref, oc_ref, ga_ref, gb_ref, gc_ref, bg_ref, wa_ref, wb_ref, wc_ref,
                  wo_ref, x_ref, mg_ref, o_ref):
    d = D_MODEL

    def branch(act_ref, w_ref, g_ref, k):
        br = jnp.dot(act_ref[0], w_ref[...], preferred_element_type=F32)
        return jax.nn.sigmoid(g_ref[0].astype(F32) + bg_ref[:, k * d:(k + 1) * d]) * br

    m = (branch(ya_ref, wa_ref, ga_ref, 0) + branch(zb_ref, wb_ref, gb_ref, 1)
         + branch(oc_ref, wc_ref, gc_ref, 2))
    out = jnp.dot(m.astype(BF16), wo_ref[...], preferred_element_type=F32)
    o_ref[0] = x_ref[0] + mg_ref[0] * out


def gated_merge(ya, zb, oc, y, b_gate, w_a_out, w_b_out, w_c_out, w_o, x, mod_gate):
    b, t, d = x.shape
    tm = min(256, t)
    per_batch = mod_gate.shape[0] > 1
    row = lambda width, cb: pl.BlockSpec((1, tm, width), lambda bi, i: (bi, i, cb))
    const = lambda shape: pl.BlockSpec(shape, lambda bi, i: (0,) * len(shape),
                                       pipeline_mode=pl.Buffered(1))
    g0 = COL_GATE // d
    return pl.pallas_call(
        _merge_kernel,
        out_shape=jax.ShapeDtypeStruct((b, t, d), F32),
        grid=(b, t // tm),
        in_specs=[row(W_CONV, 0), row(W_HYENA, 0), row(W_DIFF, 0),
                  row(d, g0), row(d, g0 + 1), row(d, g0 + 2),
                  const((1, 3 * d)), const((W_CONV, d)), const((W_HYENA, d)), const((W_DIFF, d)),
                  const((d, d)), row(d, 0),
                  pl.BlockSpec((1, 1, d), (lambda bi, i: (bi, 0, 0)) if per_batch
                               else (lambda bi, i: (0, 0, 0)))],
        out_specs=row(d, 0),
        compiler_params=_params(2),
        name="gated_merge",
    )(ya, zb, oc, y, y, y, b_gate.reshape(1, 3 * d), w_a_out.astype(BF16), w_b_out.astype(BF16),
      w_c_out.astype(BF16), w_o.astype(BF16), x, mod_gate)


def _ffn_down_kernel(*refs, tm, tk, final):
    (a_ref, v_ref, ap_ref, vp_ref, an_ref, vn_ref, cw_ref, cb_ref, w_ref, x_ref, mg_ref) = refs[:11]
    o_ref, ext_ref, hid_ref = refs[-3:]
    i = pl.program_id(1)
    last_i = pl.num_programs(1) - 1

    def conv(slot, c_ref, p_ref, n_ref, c0, w0):
        ext_ref[slot, 0:HALO, :] = jnp.where(i == 0, 0.0, p_ref[0, :, c0:c0 + tk].astype(F32))
        ext_ref[slot, HALO:HALO + tm, :] = c_ref[0, :, c0:c0 + tk].astype(F32)
        ext_ref[slot, HALO + tm:2 * HALO + tm, :] = jnp.where(
            i == last_i, 0.0, n_ref[0, :, c0:c0 + tk].astype(F32))
        w = cw_ref[:, w0:w0 + tk]
        return (w[0:1] * ext_ref[slot, HALO - 1:HALO - 1 + tm, :]
                + w[1:2] * ext_ref[slot, HALO:HALO + tm, :]
                + w[2:3] * ext_ref[slot, HALO + 1:HALO + 1 + tm, :] + cb_ref[:, w0:w0 + tk])

    out = None
    for kc in range(D_FF // tk):
        c0 = kc * tk
        a = conv((2 * kc) % 4, a_ref, ap_ref, an_ref, c0, c0)
        v = conv((2 * kc + 1) % 4, v_ref, vp_ref, vn_ref, c0, D_FF + c0)
        hid_ref[kc % 2] = (a * jax.nn.sigmoid(a) * v).astype(BF16)
        part = jnp.dot(hid_ref[kc % 2], w_ref[c0:c0 + tk, :], preferred_element_type=F32)
        out = part if out is None else out + part
    out = x_ref[0] + mg_ref[0] * out
    if final:
        out = _rms(out, refs[11][...])
    o_ref[0] = out


def ffn_down(u, conv_w, conv_b, w_down, layer, x, mod_gate, final_g=None):
    b, t, d = x.shape
    tk = 512
    tm = min(256, t)
    hb = tm // HALO
    n_hb = t // HALO
    per_batch = mod_gate.shape[0] > 1
    main = lambda cb: pl.BlockSpec((1, tm, D_FF), lambda bi, i: (bi, i, cb))
    prev = lambda cb: pl.BlockSpec((1, HALO, D_FF), lambda bi, i: (bi, jnp.maximum(i * hb - 1, 0), cb))
    nxt = lambda cb: pl.BlockSpec((1, HALO, D_FF),
                                  lambda bi, i: (bi, jnp.minimum((i + 1) * hb, n_hb - 1), cb))
    const = lambda shape: pl.BlockSpec(shape, lambda bi, i: (0,) * len(shape))
    in_specs = [main(0), main(1), prev(0), prev(1), nxt(0), nxt(1),
                const((3, 2 * D_FF)), const((1, 2 * D_FF)),
                pl.BlockSpec((None, D_FF, d), lambda bi, i: (layer, 0, 0),
                             pipeline_mode=pl.Buffered(1)),
                pl.BlockSpec((1, tm, d), lambda bi, i: (bi, i, 0)),
                pl.BlockSpec((1, 1, d), (lambda bi, i: (bi, 0, 0)) if per_batch
                             else (lambda bi, i: (0, 0, 0)))]
    args = [u, u, u, u, u, u, conv_w, conv_b.reshape(1, 2 * D_FF), w_down, x, mod_gate]
    if final_g is not None:
        in_specs.append(const((1, d)))
        args.append(final_g.reshape(1, d))
    kern = functools.partial(_ffn_down_kernel, tm=tm, tk=tk, final=final_g is not None)
    return pl.pallas_call(
        kern,
        out_shape=jax.ShapeDtypeStruct((b, t, d), F32),
        grid=(b, t // tm),
        in_specs=in_specs,
        out_specs=pl.BlockSpec((1, tm, d), lambda bi, i: (bi, i, 0)),
        scratch_shapes=[pltpu.VMEM((4, tm + 2 * HALO, tk), F32),
                        pltpu.VMEM((2, tm, tk), BF16)],
        compiler_params=_params(2),
        name="ffn_down",
    )(*args)


def mixer_sublayer(x, y, y_c, y_lat, rope, p, mod_gate, lam_init):
    ya = depthwise_conv(y, COL_A, W_CONV, p['conv_a_w'], p['conv_a_b'], glu=True,
                        ln=(p['ln_a_g'], p['ln_a_b']), out_dtype=BF16, name="conformer_conv")
    v, x1, x2 = [depthwise_conv(y, COL_B + k * W_HYENA, W_HYENA,
                                p['short_b_w'][:, k * W_HYENA:(k + 1) * W_HYENA],
                                p['short_b_b'][k * W_HYENA:(k + 1) * W_HYENA], name="hyena_short_conv")
                 for k in range(3)]
    filt = (p['filt_w1'], p['filt_b1'], p['filt_w2'], p['filt_b2'], p['filt_w3'])
    zb = hyena_long_conv(v, x1, x2, p['hyena_skip'], filt)
    oc = diff_attention(y, y_c, y_lat, rope, p['diff_lambda'], p['subln_g'], lam_init)
    return gated_merge(ya, zb, oc, y, p['b_gate'], p['w_a_out'], p['w_b_out'], p['w_c_out'],
                       p['w_o'], x, mod_gate)


def kernel(x, c, ctx, c_ctx, w_ada, b_ada, norm1_g, norm2_g, w_in, b_gate, conv_a_w, conv_a_b, ln_a_g, ln_a_b, w_a_out, short_b_w, short_b_b, filt_w1, filt_b1, filt_w2, filt_b2, filt_w3, hyena_skip, w_b_out, diff_lambda, subln_g, w_c_out, w_o, w_up, conv_f_w, conv_f_b, w_down, final_g):
    batch = x.shape[0]
    rope = rope_tables(x.shape[1])
    cc = jnp.concatenate([c, c_ctx[None, :], jnp.zeros((8 - batch - 1, D_MODEL), F32)], axis=0)
    gate_rot = N_MIX // TN
    w_in_b, w_up_b, w_down_b = w_in.astype(BF16), w_up.astype(BF16), w_down.astype(BF16)
    for l in range(DEPTH):
        p = {
            'b_gate': b_gate[l],
            'conv_a_w': conv_a_w[l], 'conv_a_b': conv_a_b[l],
            'ln_a_g': ln_a_g[l], 'ln_a_b': ln_a_b[l], 'w_a_out': w_a_out[l],
            'short_b_w': short_b_w[l], 'short_b_b': short_b_b[l],
            'filt_w1': filt_w1[l], 'filt_b1': filt_b1[l], 'filt_w2': filt_w2[l],
            'filt_b2': filt_b2[l], 'filt_w3': filt_w3[l],
            'hyena_skip': hyena_skip[l], 'w_b_out': w_b_out[l],
            'diff_lambda': diff_lambda[l], 'subln_g': subln_g[l], 'w_c_out': w_c_out[l],
            'w_o': w_o[l],
        }
        update_ctx = l < DEPTH - 1
        last = l == DEPTH - 1
        lam_init = 0.8 - 0.6 * math.exp(-0.3 * l)
        mod = ada_projection(cc, w_ada, l, b_ada[l])
        mx = [mod[:batch, None, j * D_MODEL:(j + 1) * D_MODEL] for j in range(6)]
        mc = [mod[batch:batch + 1, None, j * D_MODEL:(j + 1) * D_MODEL] for j in range(6)]

        y_x = norm_mod_matmul(x, norm1_g[l], mx[0], mx[1], w_in_b, l, gate_rot, "mm_in_x")
        t_c = ctx.shape[1]
        merged = lambda a: a.reshape(1, batch * t_c, a.shape[2])
        split = lambda a: a.reshape(batch, t_c, a.shape[2])
        y_c = split(norm_mod_matmul(merged(ctx), norm1_g[l], mc[0], mc[1], w_in_b, l, gate_rot,
                                    "mm_in_c"))
        x = mixer_sublayer(x, y_x, y_c, y_x, rope, p, mx[2], lam_init)
        u = norm_mod_matmul(x, norm2_g[l], mx[3], mx[4], w_up_b, l, 0, "mm_up_x")
        x = ffn_down(u, conv_f_w[l], conv_f_b[l], w_down_b, l, x, mx[5], final_g if last else None)
        if update_ctx:
            ctx = mixer_sublayer(ctx, y_c, y_c, None, None, p, mc[2], lam_init)
            u = split(norm_mod_matmul(merged(ctx), norm2_g[l], mc[3], mc[4], w_up_b, l, 0, "mm_up_c"))
            ctx = ffn_down(u, conv_f_w[l], conv_f_b[l], w_down_b, l, ctx, mc[5])
    return x
```

```python
import functools
import math

import numpy as np
import jax
import jax.numpy as jnp
from jax import lax
from jax.experimental import pallas as pl
from jax.experimental.pallas import tpu as pltpu

D_MODEL = 2048
DEPTH = 2
GRID_W = 64
W_CONV = D_MODEL // 4
W_HYENA = D_MODEL // 4
N_HEADS = 8
HEAD_DIM = 64
HEAD_W = 2 * HEAD_DIM
W_DIFF = N_HEADS * HEAD_W
D_FF = 5632
HYENA_ORDER = 2
HYENA_BANDS = 8
HYENA_TARGET = 1e-2
HYENA_FAST_PCT = 0.3
HYENA_SLOW_PCT = 1.5
ROPE_BASE = 10000.0
EPS = 1e-6
LOG2E = 1.4426950408889634

LANE = 128
SUBLANES = 8
HALO = 16
FREQ_BLOCK = 256
TN = 512
VMEM_LIMIT_BYTES = 56 * 1024 * 1024

N_IN = 2 * W_CONV + 3 * W_HYENA + 3 * W_DIFF + 3 * D_MODEL
N_MIX = N_IN - 3 * D_MODEL
COL_GATE = 0
COL_A = 3 * D_MODEL
COL_B = COL_A + 2 * W_CONV
COL_Q = COL_B + 3 * W_HYENA
COL_K = COL_Q + W_DIFF
COL_V = COL_K + W_DIFF

F32 = jnp.float32
BF16 = jnp.bfloat16


def _params(n_axes):
    return pltpu.CompilerParams(
        dimension_semantics=("arbitrary",) * n_axes, vmem_limit_bytes=VMEM_LIMIT_BYTES)


def _rms(x, g):
    return x * lax.rsqrt(jnp.mean(x * x, axis=-1, keepdims=True) + EPS) * g


def _ada_kernel(c_ref, w_ref, b_ref, o_ref):
    c = c_ref[...]
    h = c * jax.nn.sigmoid(c)
    o_ref[...] = jnp.dot(h, w_ref[...], precision=lax.Precision.HIGHEST,
                         preferred_element_type=F32) + b_ref[...]


def ada_projection(cc, w, layer, b):
    m, d = cc.shape
    n = w.shape[2]
    tn = 1024
    return pl.pallas_call(
        _ada_kernel,
        out_shape=jax.ShapeDtypeStruct((m, n), F32),
        grid=(n // tn,),
        in_specs=[pl.BlockSpec((m, d), lambda j: (0, 0)),
                  pl.BlockSpec((None, d, tn), lambda j: (layer, 0, j)),
                  pl.BlockSpec((1, tn), lambda j: (0, j))],
        out_specs=pl.BlockSpec((m, tn), lambda j: (0, j)),
        compiler_params=_params(1),
        name="ada_proj",
    )(cc, w, b.reshape(1, n))


def _nmm_kernel(x_ref, g_ref, sh_ref, sc_ref, w_ref, o_ref, h_ref):
    @pl.when(pl.program_id(2) == 0)
    def _():
        rb = min(256, h_ref.shape[0])

        def rows(r, carry):
            r0 = pl.multiple_of(r * rb, rb)
            y = _rms(x_ref[0, pl.ds(r0, rb), :], g_ref[...])
            h_ref[pl.ds(r0, rb), :] = (y * (1.0 + sc_ref[0]) + sh_ref[0]).astype(BF16)
            return carry

        lax.fori_loop(0, h_ref.shape[0] // rb, rows, 0)

    o_ref[0] = jnp.dot(h_ref[...], w_ref[...], preferred_element_type=F32).astype(o_ref.dtype)


def norm_mod_matmul(x, g, shift, scale, w, layer, col_rot, name):
    b, t, d = x.shape
    n = w.shape[2]
    tm = min(2048, t)
    nt = n // TN
    per_batch = shift.shape[0] > 1
    mod_spec = pl.BlockSpec((1, 1, d), (lambda bi, i, j: (bi, 0, 0)) if per_batch
                            else (lambda bi, i, j: (0, 0, 0)))
    return pl.pallas_call(
        _nmm_kernel,
        out_shape=jax.ShapeDtypeStruct((b, t, n), BF16),
        grid=(b, t // tm, nt),
        in_specs=[pl.BlockSpec((1, tm, d), lambda bi, i, j: (bi, i, 0)),
                  pl.BlockSpec((1, d), lambda bi, i, j: (0, 0)),
                  mod_spec, mod_spec,
                  pl.BlockSpec((None, d, TN), lambda bi, i, j: (layer, 0, (j + col_rot) % nt))],
        out_specs=pl.BlockSpec((1, tm, TN), lambda bi, i, j: (bi, i, j)),
        scratch_shapes=[pltpu.VMEM((tm, d), BF16)],
        compiler_params=_params(3),
        name=name,
    )(x, g.reshape(1, d), shift, scale, w)


def _dwconv_kernel(*refs, taps, glu, post, tt, rb):
    n_in = 2 if glu else 1
    main = refs[:n_in]
    prev = refs[n_in:2 * n_in]
    nxt = refs[2 * n_in:3 * n_in]
    rest = refs[3 * n_in:]
    shifted = taps > SUBLANES
    if shifted:
        rest, sh_ref = rest[:-1], rest[-1]
    if post:
        w_ref, b_ref, lg_ref, lb_ref, o_ref, ext_ref = rest
    else:
        w_ref, b_ref, o_ref, ext_ref = rest
    i = pl.program_id(1)
    last = pl.num_programs(1) - 1

    def value(blocks):
        a = blocks[0][0].astype(F32)
        if glu:
            a = a * jax.nn.sigmoid(blocks[1][0].astype(F32))
        return a

    ext_ref[0:HALO, :] = jnp.where(i == 0, 0.0, value(prev))
    ext_ref[HALO:HALO + tt, :] = value(main)
    ext_ref[HALO + tt:2 * HALO + tt, :] = jnp.where(i == last, 0.0, value(nxt))

    pad = (taps - 1) // 2
    if shifted:
        span = sh_ref.shape[1]
        for s in range(1, SUBLANES):
            sh_ref[s - 1] = ext_ref[s:s + span, :]

    def window(start):
        s = start % SUBLANES
        if not shifted or s == 0:
            return ext_ref[start:start + rb, :]
        return sh_ref[s - 1, start - s:start - s + rb, :]

    for r in range(tt // rb):
        acc = jnp.zeros((rb, ext_ref.shape[1]), F32) + b_ref[...]
        for j in range(taps):
            acc = acc + w_ref[j:j + 1, :] * window(HALO - pad + j + r * rb)
        if post:
            mu = jnp.mean(acc, axis=-1, keepdims=True)
            cen = acc - mu
            var = jnp.mean(cen * cen, axis=-1, keepdims=True)
            y = cen * lax.rsqrt(var + EPS) * lg_ref[...] + lb_ref[...]
            acc = y * jax.nn.sigmoid(y)
        o_ref[0, r * rb:(r + 1) * rb, :] = acc.astype(o_ref.dtype)


def depthwise_conv(y, col, width, w, bias, glu=False, ln=None, out_dtype=F32, name="dwconv"):
    b, t, _ = y.shape
    taps = w.shape[0]
    tt = min(512 if ln is not None else 1024, t)
    cw = width if ln is not None else min(512, width)
    ncb = width // cw
    cb0 = col // cw
    hb = tt // HALO
    n_hb = t // HALO
    cols = [cb0, cb0 + ncb] if glu else [cb0]

    def main_spec(c0):
        return pl.BlockSpec((1, tt, cw), lambda bi, i, c: (bi, i, c0 + c))

    def prev_spec(c0):
        return pl.BlockSpec((1, HALO, cw), lambda bi, i, c: (bi, jnp.maximum(i * hb - 1, 0), c0 + c))

    def next_spec(c0):
        return pl.BlockSpec((1, HALO, cw),
                            lambda bi, i, c: (bi, jnp.minimum((i + 1) * hb, n_hb - 1), c0 + c))

    vec = lambda rows: pl.BlockSpec((rows, cw), lambda bi, i, c: (0, c))
    in_specs = ([main_spec(c0) for c0 in cols] + [prev_spec(c0) for c0 in cols]
                + [next_spec(c0) for c0 in cols] + [vec(taps), vec(1)])
    args = [y] * (3 * len(cols)) + [w, bias.reshape(1, width)]
    if ln is not None:
        in_specs += [vec(1), vec(1)]
        args += [ln[0].reshape(1, width), ln[1].reshape(1, width)]
    kern = functools.partial(_dwconv_kernel, taps=taps, glu=glu, post=ln is not None, tt=tt,
                             rb=min(128, tt))
    return pl.pallas_call(
        kern,
        out_shape=jax.ShapeDtypeStruct((b, t, width), out_dtype),
        grid=(b, t // tt, ncb),
        in_specs=in_specs,
        out_specs=pl.BlockSpec((1, tt, cw), lambda bi, i, c: (bi, i, c)),
        scratch_shapes=[pltpu.VMEM((tt + 2 * HALO, cw), F32)]
        + ([pltpu.VMEM((SUBLANES - 1, tt + 2 * HALO - SUBLANES, cw), F32)] if taps > SUBLANES else []),
        compiler_params=_params(3),
        name=name,
    )(*args)


def _swap_halves(x):
    lanes = lax.broadcasted_iota(jnp.int32, x.shape, 1)
    first = (lanes % HEAD_DIM) < (HEAD_DIM // 2)
    return jnp.where(first, pltpu.roll(x, LANE - HEAD_DIM // 2, 1), pltpu.roll(x, HEAD_DIM // 2, 1))


def _attn_kernel(*refs, lam_init, n_sub, t_c, has_lat):
    if has_lat:
        (q_ref, kc_ref, vc_ref, kx_ref, vx_ref, cq_ref, sq_ref, ck_ref, sk_ref, dl_ref, g_ref,
         o_ref, s_ref, p_ref, k_s, vt_s) = refs
    else:
        q_ref, kc_ref, vc_ref, dl_ref, g_ref, o_ref, s_ref, p_ref, k_s, vt_s = refs
    t_k = k_s.shape[0]
    tq = q_ref.shape[1]
    ts = tq // n_sub
    ck = 256

    @pl.when(pl.program_id(2) == 0)
    def _():
        k_s[0:t_c, :] = kc_ref[0]
        if has_lat:
            kx = kx_ref[0].astype(F32)
            k_s[t_c:, :] = (kx * ck_ref[...] + _swap_halves(kx) * sk_ref[...]).astype(BF16)
        for c in range(t_k // ck):
            lo = c * ck
            src = vc_ref[0, lo:lo + ck, :] if lo < t_c else vx_ref[0, lo - t_c:lo - t_c + ck, :]
            vt_s[:, lo:lo + ck] = src.astype(F32).T.astype(BF16)

    q = q_ref[0].astype(F32)
    if has_lat:
        q = q * cq_ref[...] + _swap_halves(q) * sq_ref[...]
    qt = (q * (HEAD_DIM ** -0.5 * LOG2E)).T
    rows = lax.broadcasted_iota(jnp.int32, qt.shape, 0)
    qts = (jnp.where(rows < HEAD_DIM, qt, 0.0).astype(BF16),
           jnp.where(rows >= HEAD_DIM, qt, 0.0).astype(BF16))

    kk = k_s[...]
    vt = vt_s[...]
    streams = [(mi, si) for si in range(n_sub) for mi in range(2)]
    ms = {}
    for mi, si in streams:
        s = jnp.dot(kk, qts[mi][:, si * ts:(si + 1) * ts], preferred_element_type=F32)
        s_ref[mi, si] = s
        ms[mi, si] = jnp.max(s, axis=0, keepdims=True)
    outs = {}
    for mi, si in streams:
        p = jnp.exp2(s_ref[mi, si] - ms[mi, si])
        l = jnp.sum(p, axis=0, keepdims=True)
        p_ref[mi, si] = p.astype(BF16)
        acc = jnp.dot(vt, p_ref[mi, si], preferred_element_type=F32)
        outs[mi, si] = acc / l

    dl = dl_ref[...]
    s1 = jnp.sum(dl[0:1] * dl[1:2], axis=-1, keepdims=True)
    s2 = jnp.sum(dl[2:3] * dl[3:4], axis=-1, keepdims=True)
    lam = jnp.exp(s1) - jnp.exp(s2) + lam_init
    for si in range(n_sub):
        ot = outs[0, si] - lam * outs[1, si]
        ot = ot * lax.rsqrt(jnp.mean(ot * ot, axis=0, keepdims=True) + EPS)
        o = ot.T
        o_ref[0, si * ts:(si + 1) * ts, :] = (o * g_ref[0] * (1.0 - lam_init)).astype(o_ref.dtype)


def diff_attention(y_q, y_c, y_x, rope, diff_lambda, subln_g, lam_init):
    b, t_q, _ = y_q.shape
    t_c = y_c.shape[1]
    has_lat = y_x is not None
    t_k = t_c + (y_x.shape[1] if has_lat else 0)
    tq = min(512, t_q)
    ts = min(256, tq)
    n_sub = tq // ts
    qb, kb, vb = COL_Q // HEAD_W, COL_K // HEAD_W, COL_V // HEAD_W
    head = lambda base, rows: pl.BlockSpec((1, rows, HEAD_W), lambda bi, hi, i: (bi, 0, base + hi))
    in_specs = [pl.BlockSpec((1, tq, HEAD_W), lambda bi, hi, i: (bi, i, qb + hi)),
                head(kb, t_c), head(vb, t_c)]
    args = [y_q, y_c, y_c]
    if has_lat:
        t_x = y_x.shape[1]
        in_specs += [head(kb, t_x), head(vb, t_x),
                     pl.BlockSpec((tq, HEAD_W), lambda bi, hi, i: (i, 0)),
                     pl.BlockSpec((tq, HEAD_W), lambda bi, hi, i: (i, 0)),
                     pl.BlockSpec((t_x, HEAD_W), lambda bi, hi, i: (0, 0)),
                     pl.BlockSpec((t_x, HEAD_W), lambda bi, hi, i: (0, 0))]
        args += [y_x, y_x, rope[0], rope[1], rope[0], rope[1]]
    in_specs += [pl.BlockSpec((4, HEAD_DIM), lambda bi, hi, i: (0, 0)),
                 pl.BlockSpec((1, 1, HEAD_W), lambda bi, hi, i: (hi, 0, 0))]
    args += [diff_lambda.astype(F32), subln_g.astype(F32).reshape(N_HEADS, 1, HEAD_W)]
    kern = functools.partial(_attn_kernel, lam_init=lam_init, n_sub=n_sub, t_c=t_c, has_lat=has_lat)
    return pl.pallas_call(
        kern,
        out_shape=jax.ShapeDtypeStruct((b, t_q, W_DIFF), BF16),
        grid=(b, N_HEADS, t_q // tq),
        in_specs=in_specs,
        out_specs=pl.BlockSpec((1, tq, HEAD_W), lambda bi, hi, i: (bi, i, hi)),
        scratch_shapes=[pltpu.VMEM((2, n_sub, t_k, ts), F32),
                        pltpu.VMEM((2, n_sub, t_k, ts), BF16),
                        pltpu.VMEM((t_k, HEAD_W), BF16),
                        pltpu.VMEM((HEAD_W, t_k), BF16)],
        compiler_params=_params(3),
        name="diff_attn",
    )(*args)


def rope_tables(t):
    rows = t // GRID_W
    row = np.repeat(np.arange(rows), GRID_W).astype(np.float64)
    col = np.tile(np.arange(GRID_W), rows).astype(np.float64)
    n_freq = HEAD_DIM // 4
    inv = (ROPE_BASE ** (-np.arange(n_freq, dtype=np.float32) / n_freq)).astype(np.float64)
    ang = np.concatenate([row[:, None] * inv, col[:, None] * inv], axis=-1)
    c, s = np.cos(ang), np.sin(ang)
    cos_t = np.tile(np.concatenate([c, c], axis=-1), (1, 2))
    sin_t = np.tile(np.concatenate([-s, s], axis=-1), (1, 2))
    return jnp.asarray(cos_t, F32), jnp.asarray(sin_t, F32)


@functools.lru_cache(maxsize=None)
def _dft_factors(length):
    n = 2 * length
    fb = min(FREQ_BLOCK, length)
    nb = length // fb
    kk = np.arange(length, dtype=np.int64).reshape(nb, 1, fb)
    k_rows = np.concatenate([kk, kk], axis=1)
    is_sin = np.zeros((nb, 2, fb), bool)
    is_sin[:, 1, :] = True
    k_rows[0, 1, 0] = length
    is_sin[0, 1, 0] = False
    k_rows = k_rows.reshape(-1)
    is_sin = is_sin.reshape(-1)
    n_hi = max(length // LANE, 1)
    n_lo = min(LANE, length)
    a_lo = 2.0 * np.pi * ((k_rows[:, None] * np.arange(n_lo)[None, :]) % n) / n
    a_hi = 2.0 * np.pi * ((k_rows[:, None] * (LANE * np.arange(n_hi))[None, :]) % n) / n
    c_lo, s_lo, c_hi, s_hi = np.cos(a_lo), np.sin(a_lo), np.cos(a_hi), np.sin(a_hi)
    p = np.where(is_sin[:, None], s_hi, c_hi)
    q = np.where(is_sin[:, None], c_hi, -s_hi)
    scale = np.full((2 * length,), 2.0 / n)
    scale[0] = 1.0 / n
    scale[fb] = 1.0 / n
    f32 = lambda a: np.asarray(a, np.float32)
    return f32(p), f32(q), f32(c_lo), f32(s_lo), f32(scale.reshape(-1, 1))


def dft_tables(length):
    p, q, c_lo, s_lo, _ = _dft_factors(length)
    fwd = (jnp.asarray(p)[:, :, None] * jnp.asarray(c_lo)[:, None, :]
           + jnp.asarray(q)[:, :, None] * jnp.asarray(s_lo)[:, None, :])
    fwd = fwd.reshape(2 * length, length).astype(BF16)
    return fwd, fwd.T


def _cmul(za, zb, ha, hb, first_rows):
    ya = za * ha - zb * hb
    yb = za * hb + zb * ha
    ya = jnp.where(first_rows, za * ha, ya)
    yb = jnp.where(first_rows, zb * hb, yb)
    return ya, yb


def _fwd_kernel(f_ref, z_ref, h_ref, c_ref, y_ref, zb_ref, *, fb):
    @pl.when(pl.program_id(1) == 0)
    def _():
        zb_ref[...] = z_ref[0].astype(BF16)

    spec = jnp.dot(f_ref[...], zb_ref[...], preferred_element_type=F32)
    za, zb = spec[:fb], spec[fb:]
    h = h_ref[...]
    rows = lax.broadcasted_iota(jnp.int32, za.shape, 0)
    first = jnp.logical_and(rows == 0, pl.program_id(1) == 0)
    ya, yb = _cmul(za, zb, h[:fb], h[fb:], first)
    c = c_ref[...]
    y_ref[0, :fb, :] = (ya * c[:fb]).astype(y_ref.dtype)
    y_ref[0, fb:, :] = (yb * c[fb:]).astype(y_ref.dtype)


def hyena_forward(z, z_col, hspec, order, fwd, length):
    b = z.shape[0]
    w = W_HYENA
    scale = jnp.asarray(_dft_factors(length)[4])
    fb = min(FREQ_BLOCK, length)
    kern = functools.partial(_fwd_kernel, fb=fb)
    return pl.pallas_call(
        kern,
        out_shape=jax.ShapeDtypeStruct((b, 2 * length, w), BF16),
        grid=(b, length // fb),
        in_specs=[pl.BlockSpec((2 * fb, length), lambda bi, i: (i, 0)),
                  pl.BlockSpec((1, length, w), lambda bi, i: (bi, 0, z_col)),
                  pl.BlockSpec((2 * fb, w), lambda bi, i: (i, order)),
                  pl.BlockSpec((2 * fb, 1), lambda bi, i: (i, 0))],
        out_specs=pl.BlockSpec((1, 2 * fb, w), lambda bi, i: (bi, i, 0)),
        scratch_shapes=[pltpu.VMEM((length, w), BF16)],
        compiler_params=_params(2),
        name="hyena_fwd",
    )(fwd, z, hspec, scale)


def _inv_kernel(g_ref, y_ref, z_ref, gate_ref, skip_ref, o_ref):
    conv = jnp.dot(g_ref[...], y_ref[0], preferred_element_type=F32)
    o_ref[0] = (gate_ref[0] * (conv + z_ref[0] * skip_ref[...])).astype(o_ref.dtype)


def hyena_inverse(y, z, z_col, gate, gate_col, skip, inv, length, out_dtype):
    b = z.shape[0]
    w = W_HYENA
    tm = min(512, length)
    return pl.pallas_call(
        _inv_kernel,
        out_shape=jax.ShapeDtypeStruct((b, length, w), out_dtype),
        grid=(b, length // tm),
        in_specs=[pl.BlockSpec((tm, 2 * length), lambda bi, i: (i, 0)),
                  pl.BlockSpec((1, 2 * length, w), lambda bi, i: (bi, 0, 0)),
                  pl.BlockSpec((1, tm, w), lambda bi, i: (bi, i, z_col)),
                  pl.BlockSpec((1, tm, w), lambda bi, i: (bi, i, gate_col)),
                  pl.BlockSpec((1, w), lambda bi, i: (0, 0))],
        out_specs=pl.BlockSpec((1, tm, w), lambda bi, i: (bi, i, 0)),
        compiler_params=_params(2),
        name="hyena_inv",
    )(inv, y, z, gate, skip.reshape(1, w))


def _filt_kernel(feat_ref, w1_ref, b1_ref, w2_ref, b2_ref, w3f_ref, w3b_ref, dl_ref,
                 taps_ref, hf_ref, hb_ref, *, rb):
    hi = lax.Precision.HIGHEST
    length, ct = hf_ref.shape
    n_blocks = length // rb

    def taps(i, norm):
        r0 = pl.multiple_of(i * rb, rb)
        feats = feat_ref[pl.ds(r0, rb), :]
        h = jnp.sin(jnp.dot(feats, w1_ref[...], precision=hi, preferred_element_type=F32)
                    + b1_ref[...])
        h = jnp.sin(jnp.dot(h, w2_ref[...], precision=hi, preferred_element_type=F32)
                    + b2_ref[...])
        decay = jnp.exp(-feats[:, 0:1] * dl_ref[...])
        hf = jnp.dot(h, w3f_ref[...], precision=hi, preferred_element_type=F32) * decay
        hb = jnp.dot(h, w3b_ref[...], precision=hi, preferred_element_type=F32) * decay
        rows = lax.broadcasted_iota(jnp.int32, hb.shape, 0) + r0
        hb = jnp.where(rows == 0, 0.0, hb)
        hf_ref[pl.ds(r0, rb), :] = hf
        hb_ref[pl.ds(r0, rb), :] = hb
        return (norm + jnp.sum(jnp.abs(hf), axis=0, keepdims=True)
                + jnp.sum(jnp.abs(hb), axis=0, keepdims=True))

    norm = lax.fori_loop(0, n_blocks, taps, jnp.zeros((1, ct), F32))
    inv = 1.0 / norm

    def normalise(i, carry):
        r0 = pl.multiple_of(i * rb, rb)
        taps_ref[0, pl.ds(r0, rb), :] = hf_ref[pl.ds(r0, rb), :] * inv
        taps_ref[1, pl.ds(r0, rb), :] = hb_ref[pl.ds(r0, rb), :] * inv
        return carry

    lax.fori_loop(0, n_blocks, normalise, 0)


def hyena_filter_taps(length, filt_w1, filt_b1, filt_w2, filt_b2, filt_w3):
    w = W_HYENA
    ct = 256
    pos = np.arange(length, dtype=np.float64)
    t = pos / max(length - 1, 1)
    bands = np.linspace(1e-4, HYENA_BANDS - 1, HYENA_BANDS)
    ang = (2.0 * math.pi * pos / length)[:, None] * bands[None, :]
    feats = np.concatenate([t[:, None], np.cos(ang), -np.sin(ang)], axis=-1).astype(np.float32)
    deltas = np.abs(np.linspace(math.log(HYENA_TARGET) / HYENA_SLOW_PCT,
                                math.log(HYENA_TARGET) / HYENA_FAST_PCT, w)).astype(np.float32)
    emb, nf = filt_w1.shape
    npc = w // ct
    full = lambda shape: pl.BlockSpec(shape, lambda o, c: (0,) * len(shape))
    kern = functools.partial(_filt_kernel, rb=min(256, length))
    return pl.pallas_call(
        kern,
        out_shape=jax.ShapeDtypeStruct((HYENA_ORDER, 2, length, w), F32),
        grid=(HYENA_ORDER, npc),
        in_specs=[full((length, emb)), full((emb, nf)), full((1, nf)), full((nf, nf)), full((1, nf)),
                  pl.BlockSpec((nf, ct), lambda o, c: (0, o * 2 * npc + c)),
                  pl.BlockSpec((nf, ct), lambda o, c: (0, o * 2 * npc + npc + c)),
                  pl.BlockSpec((1, ct), lambda o, c: (0, c))],
        out_specs=pl.BlockSpec((None, 2, length, ct), lambda o, c: (o, 0, 0, c)),
        scratch_shapes=[pltpu.VMEM((length, ct), F32),
                        pltpu.VMEM((length, ct), F32)],
        compiler_params=_params(2),
        name="hyena_filt",
    )(jnp.asarray(feats), filt_w1, filt_b1.reshape(1, nf), filt_w2, filt_b2.reshape(1, nf),
      filt_w3, filt_w3, jnp.asarray(deltas).reshape(1, w))


def _hspec_kernel(f_ref, t_ref, o_ref, *, fb):
    f = f_ref[...]
    hs = (t_ref[0] + t_ref[1]).astype(BF16)
    hd = (t_ref[0] - t_ref[1]).astype(BF16)
    a = jnp.dot(f, hs, preferred_element_type=F32)
    bd = jnp.dot(f[fb:], hd, preferred_element_type=F32)
    rows = lax.broadcasted_iota(jnp.int32, bd.shape, 0)
    first = jnp.logical_and(rows == 0, pl.program_id(1) == 0)
    o_ref[:fb, :] = a[:fb]
    o_ref[fb:, :] = jnp.where(first, a[fb:], bd)


def hyena_filter_spectrum(taps, fwd, length):
    fb = min(FREQ_BLOCK, length)
    w = W_HYENA
    kern = functools.partial(_hspec_kernel, fb=fb)
    return pl.pallas_call(
        kern,
        out_shape=jax.ShapeDtypeStruct((2 * length, HYENA_ORDER * w), F32),
        grid=(HYENA_ORDER, length // fb),
        in_specs=[pl.BlockSpec((2 * fb, length), lambda j, i: (i, 0)),
                  pl.BlockSpec((None, 2, length, w), lambda j, i: (j, 0, 0, 0))],
        out_specs=pl.BlockSpec((2 * fb, w), lambda j, i: (i, j)),
        compiler_params=_params(2),
        name="hyena_hspec",
    )(fwd, taps)


FFT_N1 = 64
FFT_N2 = 128
FFT_N = FFT_N1 * FFT_N2
FFT_KB = 8
FFT_TN = 8192


@functools.lru_cache(maxsize=None)
def _fft_tables():
    n1h = FFT_N1 // 2
    k1 = np.arange(FFT_N1)
    ang = 2.0 * np.pi * np.outer(k1, np.arange(n1h)) / FFT_N1
    c, s = np.cos(ang), np.sin(ang)
    f1 = np.block([[c, s], [-s, c]])
    f1i = np.block([[c.T, -s.T], [s.T, c.T]]) / FFT_N
    k2 = np.arange(FFT_N2)
    n2 = np.arange(FFT_N2)
    kfull = k1[:, None, None] + FFT_N1 * k2[None, :, None]
    ang2 = 2.0 * np.pi * ((kfull * n2[None, None, :]) % FFT_N) / FFT_N
    gc, gs = np.cos(ang2), np.sin(ang2)
    g = np.concatenate([np.concatenate([gc, gs], axis=2),
                        np.concatenate([-gs, gc], axis=2)], axis=1)
    f32 = lambda a: np.asarray(a, np.float32)
    return f32(f1), f32(g), f32(np.transpose(g, (0, 2, 1))), f32(f1i)


def _fft_s1_kernel(f_ref, x_ref, o_ref):
    o_ref[0] = jnp.dot(f_ref[...], x_ref[0].astype(BF16),
                       preferred_element_type=F32).astype(o_ref.dtype)


def fft_stage1(x):
    p, rows, cols = x.shape
    f1 = jnp.asarray(_fft_tables()[0][:, :rows], BF16)
    return pl.pallas_call(
        _fft_s1_kernel,
        out_shape=jax.ShapeDtypeStruct((p, 2 * FFT_N1, cols), BF16),
        grid=(p, cols // FFT_TN),
        in_specs=[pl.BlockSpec((2 * FFT_N1, rows), lambda pi, j: (0, 0)),
                  pl.BlockSpec((1, rows, FFT_TN), lambda pi, j: (pi, 0, j))],
        out_specs=pl.BlockSpec((1, 2 * FFT_N1, FFT_TN), lambda pi, j: (pi, 0, j)),
        compiler_params=_params(2),
        name="fft_s1",
    )(f1, x)


def _fft_s2(g_ref, a_ref, j):
    rhs = jnp.concatenate([a_ref[0, 0, j], a_ref[0, 1, j]], axis=0)
    return jnp.dot(g_ref[j], rhs, preferred_element_type=F32)


def _fft_s2_filter_kernel(g_ref, ap_ref, aq_ref, h_ref):
    half = FFT_N2
    for j in range(FFT_KB):
        sp = _fft_s2(g_ref, ap_ref, j)
        sq = _fft_s2(g_ref, aq_ref, j)
        h_ref[0, j, :half, :] = sp[:half] + sq[:half]
        h_ref[0, j, half:, :] = sp[half:] - sq[half:]


def _fft_s2_data_kernel(g_ref, a_ref, h_ref, y_ref):
    half = FFT_N2
    for j in range(FFT_KB):
        s = _fft_s2(g_ref, a_ref, j)
        sre, sim = s[:half], s[half:]
        hre, him = h_ref[0, j, :half, :], h_ref[0, j, half:, :]
        y_ref[0, j, :half, :] = (sre * hre - sim * him).astype(y_ref.dtype)
        y_ref[0, j, half:, :] = (sre * him + sim * hre).astype(y_ref.dtype)


def fft_filter_spectrum(a):
    w = W_HYENA
    g = jnp.asarray(_fft_tables()[1], BF16)
    a5 = a.reshape(2 * HYENA_ORDER, 2, FFT_N1, FFT_N2, w)
    taps_spec = lambda d: pl.BlockSpec((1, 2, FFT_KB, FFT_N2, w), lambda j, o: (2 * o + d, 0, j, 0, 0))
    return pl.pallas_call(
        _fft_s2_filter_kernel,
        out_shape=jax.ShapeDtypeStruct((HYENA_ORDER, FFT_N1, 2 * FFT_N2, w), F32),
        grid=(FFT_N1 // FFT_KB, HYENA_ORDER),
        in_specs=[pl.BlockSpec((FFT_KB, 2 * FFT_N2, 2 * FFT_N2), lambda j, o: (j, 0, 0)),
                  taps_spec(0), taps_spec(1)],
        out_specs=pl.BlockSpec((1, FFT_KB, 2 * FFT_N2, w), lambda j, o: (o, j, 0, 0)),
        compiler_params=_params(2),
        name="fft_s2_filter",
    )(g, a5, a5)


def fft_stage2(a, hspec, order):
    p = a.shape[0]
    w = W_HYENA
    g = jnp.asarray(_fft_tables()[1], BF16)
    a5 = a.reshape(p, 2, FFT_N1, FFT_N2, w)
    return pl.pallas_call(
        _fft_s2_data_kernel,
        out_shape=jax.ShapeDtypeStruct((p, FFT_N1, 2 * FFT_N2, w), BF16),
        grid=(FFT_N1 // FFT_KB, p),
        in_specs=[pl.BlockSpec((FFT_KB, 2 * FFT_N2, 2 * FFT_N2), lambda j, pi: (j, 0, 0)),
                  pl.BlockSpec((1, 2, FFT_KB, FFT_N2, w), lambda j, pi: (pi, 0, j, 0, 0)),
                  pl.BlockSpec((1, FFT_KB, 2 * FFT_N2, w), lambda j, pi: (order, j, 0, 0))],
        out_specs=pl.BlockSpec((1, FFT_KB, 2 * FFT_N2, w), lambda j, pi: (pi, j, 0, 0)),
        compiler_params=_params(2),
        name="fft_s2",
    )(g, a5, hspec)


def _ifft_a_kernel(gt_ref, y_ref, b_ref):
    half = FFT_N2
    for j in range(FFT_KB):
        b = jnp.dot(gt_ref[j], y_ref[0, j], preferred_element_type=F32)
        b_ref[0, 0, j] = b[:half].astype(b_ref.dtype)
        b_ref[0, 1, j] = b[half:].astype(b_ref.dtype)


def ifft_stage_a(y):
    p = y.shape[0]
    w = W_HYENA
    gt = jnp.asarray(_fft_tables()[2], BF16)
    out = pl.pallas_call(
        _ifft_a_kernel,
        out_shape=jax.ShapeDtypeStruct((p, 2, FFT_N1, FFT_N2, w), BF16),
        grid=(FFT_N1 // FFT_KB, p),
        in_specs=[pl.BlockSpec((FFT_KB, 2 * FFT_N2, 2 * FFT_N2), lambda j, pi: (j, 0, 0)),
                  pl.BlockSpec((1, FFT_KB, 2 * FFT_N2, w), lambda j, pi: (pi, j, 0, 0))],
        out_specs=pl.BlockSpec((1, 2, FFT_KB, FFT_N2, w), lambda j, pi: (pi, 0, j, 0, 0)),
        compiler_params=_params(2),
        name="ifft_a",
    )(gt, y)
    return out.reshape(p, 2 * FFT_N1, FFT_N2 * w)


def _ifft_b_kernel(f_ref, b_ref, z_ref, gate_ref, skip_ref, o_ref):
    conv = jnp.dot(f_ref[...], b_ref[0], preferred_element_type=F32)
    o_ref[0] = (gate_ref[0] * (conv + z_ref[0] * skip_ref[...])).astype(o_ref.dtype)


def ifft_stage_b(b, z, gate, skip, out_dtype):
    p, _, cols = b.shape
    f1i = jnp.asarray(_fft_tables()[3], BF16)
    skip_flat = jnp.tile(skip.reshape(1, W_HYENA), (1, FFT_N2))
    return pl.pallas_call(
        _ifft_b_kernel,
        out_shape=jax.ShapeDtypeStruct((p, FFT_N1, cols), out_dtype),
        grid=(p, cols // FFT_TN),
        in_specs=[pl.BlockSpec((FFT_N1, 2 * FFT_N1), lambda pi, j: (0, 0)),
                  pl.BlockSpec((1, 2 * FFT_N1, FFT_TN), lambda pi, j: (pi, 0, j)),
                  pl.BlockSpec((1, FFT_N1, FFT_TN), lambda pi, j: (pi, 0, j)),
                  pl.BlockSpec((1, FFT_N1, FFT_TN), lambda pi, j: (pi, 0, j)),
                  pl.BlockSpec((1, FFT_TN), lambda pi, j: (0, j))],
        out_specs=pl.BlockSpec((1, FFT_N1, FFT_TN), lambda pi, j: (pi, 0, j)),
        compiler_params=_params(2),
        name="ifft_b",
    )(f1i, b, z, gate, skip_flat)


def hyena_long_conv(v, x1, x2, hyena_skip, filt):
    b, length, w = v.shape
    taps = hyena_filter_taps(length, *filt)
    if 2 * length == FFT_N and b % 2 == 0:
        pairs = b // 2
        flat = lambda t: t.reshape(pairs, FFT_N1, FFT_N2 * w)
        hspec = fft_filter_spectrum(
            fft_stage1(taps.reshape(2 * HYENA_ORDER, FFT_N1 // 2, FFT_N2 * w)))
        z = flat(v)
        for o, gate in enumerate((x1, x2)):
            y = fft_stage2(fft_stage1(z), hspec, o)
            z = ifft_stage_b(ifft_stage_a(y), z, flat(gate), hyena_skip[o],
                             F32 if o == 0 else BF16)
        return z.reshape(b, length, w)
    fwd, inv = dft_tables(length)
    hspec = hyena_filter_spectrum(taps, fwd, length)
    y = hyena_forward(v, 0, hspec, 0, fwd, length)
    z = hyena_inverse(y, v, 0, x1, 0, hyena_skip[0], inv, length, F32)
    y = hyena_forward(z, 0, hspec, 1, fwd, length)
    return hyena_inverse(y, z, 0, x2, 0, hyena_skip[1], inv, length, BF16)


def _merge_kernel(ya_ref, zb_ref, oc_ref, ga_ref, gb_ref, gc_ref, bg_ref, wa_ref, wb_ref, wc_ref,
                  wo_ref, x_ref, mg_ref, o_ref):
    d = D_MODEL

    def branch(act_ref, w_ref, g_ref, k):
        br = jnp.dot(act_ref[0], w_ref[...], preferred_element_type=F32)
        return jax.nn.sigmoid(g_ref[0].astype(F32) + bg_ref[:, k * d:(k + 1) * d]) * br

    m = (branch(ya_ref, wa_ref, ga_ref, 0) + branch(zb_ref, wb_ref, gb_ref, 1)
         + branch(oc_ref, wc_ref, gc_ref, 2))
    out = jnp.dot(m.astype(BF16), wo_ref[...], preferred_element_type=F32)
    o_ref[0] = x_ref[0] + mg_ref[0] * out


def gated_merge(ya, zb, oc, y, b_gate, w_a_out, w_b_out, w_c_out, w_o, x, mod_gate):
    b, t, d = x.shape
    tm = min(256, t)
    per_batch = mod_gate.shape[0] > 1
    row = lambda width, cb: pl.BlockSpec((1, tm, width), lambda bi, i: (bi, i, cb))
    const = lambda shape: pl.BlockSpec(shape, lambda bi, i: (0,) * len(shape),
                                       pipeline_mode=pl.Buffered(1))
    g0 = COL_GATE // d
    return pl.pallas_call(
        _merge_kernel,
        out_shape=jax.ShapeDtypeStruct((b, t, d), F32),
        grid=(b, t // tm),
        in_specs=[row(W_CONV, 0), row(W_HYENA, 0), row(W_DIFF, 0),
                  row(d, g0), row(d, g0 + 1), row(d, g0 + 2),
                  const((1, 3 * d)), const((W_CONV, d)), const((W_HYENA, d)), const((W_DIFF, d)),
                  const((d, d)), row(d, 0),
                  pl.BlockSpec((1, 1, d), (lambda bi, i: (bi, 0, 0)) if per_batch
                               else (lambda bi, i: (0, 0, 0)))],
        out_specs=row(d, 0),
        compiler_params=_params(2),
        name="gated_merge",
    )(ya, zb, oc, y, y, y, b_gate.reshape(1, 3 * d), w_a_out.astype(BF16), w_b_out.astype(BF16),
      w_c_out.astype(BF16), w_o.astype(BF16), x, mod_gate)


def _ffn_down_kernel(*refs, tm, tk, final):
    (a_ref, v_ref, ap_ref, vp_ref, an_ref, vn_ref, cw_ref, cb_ref, w_ref, x_ref, mg_ref) = refs[:11]
    o_ref, ext_ref, hid_ref = refs[-3:]
    i = pl.program_id(1)
    last_i = pl.num_programs(1) - 1

    def conv(slot, c_ref, p_ref, n_ref, c0, w0):
        ext_ref[slot, 0:HALO, :] = jnp.where(i == 0, 0.0, p_ref[0, :, c0:c0 + tk].astype(F32))
        ext_ref[slot, HALO:HALO + tm, :] = c_ref[0, :, c0:c0 + tk].astype(F32)
        ext_ref[slot, HALO + tm:2 * HALO + tm, :] = jnp.where(
            i == last_i, 0.0, n_ref[0, :, c0:c0 + tk].astype(F32))
        w = cw_ref[:, w0:w0 + tk]
        return (w[0:1] * ext_ref[slot, HALO - 1:HALO - 1 + tm, :]
                + w[1:2] * ext_ref[slot, HALO:HALO + tm, :]
                + w[2:3] * ext_ref[slot, HALO + 1:HALO + 1 + tm, :] + cb_ref[:, w0:w0 + tk])

    out = None
    for kc in range(D_FF // tk):
        c0 = kc * tk
        a = conv((2 * kc) % 4, a_ref, ap_ref, an_ref, c0, c0)
        v = conv((2 * kc + 1) % 4, v_ref, vp_ref, vn_ref, c0, D_FF + c0)
        hid_ref[kc % 2] = (a * jax.nn.sigmoid(a) * v).astype(BF16)
        part = jnp.dot(hid_ref[kc % 2], w_ref[c0:c0 + tk, :], preferred_element_type=F32)
        out = part if out is None else out + part
    out = x_ref[0] + mg_ref[0] * out
    if final:
        out = _rms(out, refs[11][...])
    o_ref[0] = out


def ffn_down(u, conv_w, conv_b, w_down, layer, x, mod_gate, final_g=None):
    b, t, d = x.shape
    tk = 512
    tm = min(256, t)
    hb = tm // HALO
    n_hb = t // HALO
    per_batch = mod_gate.shape[0] > 1
    main = lambda cb: pl.BlockSpec((1, tm, D_FF), lambda bi, i: (bi, i, cb))
    prev = lambda cb: pl.BlockSpec((1, HALO, D_FF), lambda bi, i: (bi, jnp.maximum(i * hb - 1, 0), cb))
    nxt = lambda cb: pl.BlockSpec((1, HALO, D_FF),
                                  lambda bi, i: (bi, jnp.minimum((i + 1) * hb, n_hb - 1), cb))
    const = lambda shape: pl.BlockSpec(shape, lambda bi, i: (0,) * len(shape))
    in_specs = [main(0), main(1), prev(0), prev(1), nxt(0), nxt(1),
                const((3, 2 * D_FF)), const((1, 2 * D_FF)),
                pl.BlockSpec((None, D_FF, d), lambda bi, i: (layer, 0, 0),
                             pipeline_mode=pl.Buffered(1)),
                pl.BlockSpec((1, tm, d), lambda bi, i: (bi, i, 0)),
                pl.BlockSpec((1, 1, d), (lambda bi, i: (bi, 0, 0)) if per_batch
                             else (lambda bi, i: (0, 0, 0)))]
    args = [u, u, u, u, u, u, conv_w, conv_b.reshape(1, 2 * D_FF), w_down, x, mod_gate]
    if final_g is not None:
        in_specs.append(const((1, d)))
        args.append(final_g.reshape(1, d))
    kern = functools.partial(_ffn_down_kernel, tm=tm, tk=tk, final=final_g is not None)
    return pl.pallas_call(
        kern,
        out_shape=jax.ShapeDtypeStruct((b, t, d), F32),
        grid=(b, t // tm),
        in_specs=in_specs,
        out_specs=pl.BlockSpec((1, tm, d), lambda bi, i: (bi, i, 0)),
        scratch_shapes=[pltpu.VMEM((4, tm + 2 * HALO, tk), F32),
                        pltpu.VMEM((2, tm, tk), BF16)],
        compiler_params=_params(2),
        name="ffn_down",
    )(*args)


def mixer_sublayer(x, y, y_c, y_lat, rope, p, mod_gate, lam_init):
    ya = depthwise_conv(y, COL_A, W_CONV, p['conv_a_w'], p['conv_a_b'], glu=True,
                        ln=(p['ln_a_g'], p['ln_a_b']), out_dtype=BF16, name="conformer_conv")
    v, x1, x2 = [depthwise_conv(y, COL_B + k * W_HYENA, W_HYENA,
                                p['short_b_w'][:, k * W_HYENA:(k + 1) * W_HYENA],
                                p['short_b_b'][k * W_HYENA:(k + 1) * W_HYENA], name="hyena_short_conv")
                 for k in range(3)]
    filt = (p['filt_w1'], p['filt_b1'], p['filt_w2'], p['filt_b2'], p['filt_w3'])
    zb = hyena_long_conv(v, x1, x2, p['hyena_skip'], filt)
    oc = diff_attention(y, y_c, y_lat, rope, p['diff_lambda'], p['subln_g'], lam_init)
    return gated_merge(ya, zb, oc, y, p['b_gate'], p['w_a_out'], p['w_b_out'], p['w_c_out'],
                       p['w_o'], x, mod_gate)


def kernel(x, c, ctx, c_ctx, w_ada, b_ada, norm1_g, norm2_g, w_in, b_gate, conv_a_w, conv_a_b, ln_a_g, ln_a_b, w_a_out, short_b_w, short_b_b, filt_w1, filt_b1, filt_w2, filt_b2, filt_w3, hyena_skip, w_b_out, diff_lambda, subln_g, w_c_out, w_o, w_up, conv_f_w, conv_f_b, w_down, final_g):
    batch = x.shape[0]
    rope = rope_tables(x.shape[1])
    cc = jnp.concatenate([c, c_ctx[None, :], jnp.zeros((8 - batch - 1, D_MODEL), F32)], axis=0)
    gate_rot = N_MIX // TN
    w_in_b, w_up_b, w_down_b = w_in.astype(BF16), w_up.astype(BF16), w_down.astype(BF16)
    for l in range(DEPTH):
        p = {
            'b_gate': b_gate[l],
            'conv_a_w': conv_a_w[l], 'conv_a_b': conv_a_b[l],
            'ln_a_g': ln_a_g[l], 'ln_a_b': ln_a_b[l], 'w_a_out': w_a_out[l],
            'short_b_w': short_b_w[l], 'short_b_b': short_b_b[l],
            'filt_w1': filt_w1[l], 'filt_b1': filt_b1[l], 'filt_w2': filt_w2[l],
            'filt_b2': filt_b2[l], 'filt_w3': filt_w3[l],
            'hyena_skip': hyena_skip[l], 'w_b_out': w_b_out[l],
            'diff_lambda': diff_lambda[l], 'subln_g': subln_g[l], 'w_c_out': w_c_out[l],
            'w_o': w_o[l],
        }
        update_ctx = l < DEPTH - 1
        last = l == DEPTH - 1
        lam_init = 0.8 - 0.6 * math.exp(-0.3 * l)
        mod = ada_projection(cc, w_ada, l, b_ada[l])
        mx = [mod[:batch, None, j * D_MODEL:(j + 1) * D_MODEL] for j in range(6)]
        mc = [mod[batch:batch + 1, None, j * D_MODEL:(j + 1) * D_MODEL] for j in range(6)]

        y_x = norm_mod_matmul(x, norm1_g[l], mx[0], mx[1], w_in_b, l, gate_rot, "mm_in_x")
        t_c = ctx.shape[1]
        merged = lambda a: a.reshape(1, batch * t_c, a.shape[2])
        split = lambda a: a.reshape(batch, t_c, a.shape[2])
        y_c = split(norm_mod_matmul(merged(ctx), norm1_g[l], mc[0], mc[1], w_in_b, l, gate_rot,
                                    "mm_in_c"))
        x = mixer_sublayer(x, y_x, y_c, y_x, rope, p, mx[2], lam_init)
        u = norm_mod_matmul(x, norm2_g[l], mx[3], mx[4], w_up_b, l, 0, "mm_up_x")
        x = ffn_down(u, conv_f_w[l], conv_f_b[l], w_down_b, l, x, mx[5], final_g if last else None)
        if update_ctx:
            ctx = mixer_sublayer(ctx, y_c, y_c, None, None, p, mc[2], lam_init)
            u = split(norm_mod_matmul(merged(ctx), norm2_g[l], mc[3], mc[4], w_up_b, l, 0, "mm_up_c"))
            ctx = ffn_down(u, conv_f_w[l], conv_f_b[l], w_down_b, l, ctx, mc[5])
    return x
```

```python
import functools
import math

import numpy as np
import jax
import jax.numpy as jnp
from jax import lax
from jax.experimental import pallas as pl
from jax.experimental.pallas import tpu as pltpu

D_MODEL = 2048
DEPTH = 2
GRID_W = 64
W_CONV = D_MODEL // 4
W_HYENA = D_MODEL // 4
N_HEADS = 8
HEAD_DIM = 64
HEAD_W = 2 * HEAD_DIM
W_DIFF = N_HEADS * HEAD_W
D_FF = 5632
HYENA_ORDER = 2
HYENA_BANDS = 8
HYENA_TARGET = 1e-2
HYENA_FAST_PCT = 0.3
HYENA_SLOW_PCT = 1.5
ROPE_BASE = 10000.0
EPS = 1e-6
LOG2E = 1.4426950408889634

LANE = 128
SUBLANES = 8
HALO = 16
FREQ_BLOCK = 256
TN = 512
VMEM_LIMIT_BYTES = 56 * 1024 * 1024

N_IN = 2 * W_CONV + 3 * W_HYENA + 3 * W_DIFF + 3 * D_MODEL
N_MIX = N_IN - 3 * D_MODEL
COL_GATE = 0
COL_A = 3 * D_MODEL
COL_B = COL_A + 2 * W_CONV
COL_Q = COL_B + 3 * W_HYENA
COL_K = COL_Q + W_DIFF
COL_V = COL_K + W_DIFF

F32 = jnp.float32
BF16 = jnp.bfloat16


def _params(n_axes):
    return pltpu.CompilerParams(
        dimension_semantics=("arbitrary",) * n_axes, vmem_limit_bytes=VMEM_LIMIT_BYTES)


def _rms(x, g):
    return x * lax.rsqrt(jnp.mean(x * x, axis=-1, keepdims=True) + EPS) * g


def _ada_kernel(c_ref, w_ref, b_ref, o_ref):
    c = c_ref[...]
    h = c * jax.nn.sigmoid(c)
    o_ref[...] = jnp.dot(h, w_ref[...], precision=lax.Precision.HIGHEST,
                         preferred_element_type=F32) + b_ref[...]


def ada_projection(cc, w, layer, b):
    m, d = cc.shape
    n = w.shape[2]
    tn = 1024
    return pl.pallas_call(
        _ada_kernel,
        out_shape=jax.ShapeDtypeStruct((m, n), F32),
        grid=(n // tn,),
        in_specs=[pl.BlockSpec((m, d), lambda j: (0, 0)),
                  pl.BlockSpec((None, d, tn), lambda j: (layer, 0, j)),
                  pl.BlockSpec((1, tn), lambda j: (0, j))],
        out_specs=pl.BlockSpec((m, tn), lambda j: (0, j)),
        compiler_params=_params(1),
        name="ada_proj",
    )(cc, w, b.reshape(1, n))


def _nmm_kernel(x_ref, g_ref, sh_ref, sc_ref, w_ref, o_ref, h_ref):
    @pl.when(pl.program_id(2) == 0)
    def _():
        rb = min(256, h_ref.shape[0])

        def rows(r, carry):
            r0 = pl.multiple_of(r * rb, rb)
            y = _rms(x_ref[0, pl.ds(r0, rb), :], g_ref[...])
            h_ref[pl.ds(r0, rb), :] = (y * (1.0 + sc_ref[0]) + sh_ref[0]).astype(BF16)
            return carry

        lax.fori_loop(0, h_ref.shape[0] // rb, rows, 0)

    o_ref[0] = jnp.dot(h_ref[...], w_ref[...], preferred_element_type=F32).astype(o_ref.dtype)


def norm_mod_matmul(x, g, shift, scale, w, layer, col_rot, name):
    b, t, d = x.shape
    n = w.shape[2]
    tm = min(2048, t)
    nt = n // TN
    per_batch = shift.shape[0] > 1
    mod_spec = pl.BlockSpec((1, 1, d), (lambda bi, i, j: (bi, 0, 0)) if per_batch
                            else (lambda bi, i, j: (0, 0, 0)))
    return pl.pallas_call(
        _nmm_kernel,
        out_shape=jax.ShapeDtypeStruct((b, t, n), BF16),
        grid=(b, t // tm, nt),
        in_specs=[pl.BlockSpec((1, tm, d), lambda bi, i, j: (bi, i, 0)),
                  pl.BlockSpec((1, d), lambda bi, i, j: (0, 0)),
                  mod_spec, mod_spec,
                  pl.BlockSpec((None, d, TN), lambda bi, i, j: (layer, 0, (j + col_rot) % nt))],
        out_specs=pl.BlockSpec((1, tm, TN), lambda bi, i, j: (bi, i, j)),
        scratch_shapes=[pltpu.VMEM((tm, d), BF16)],
        compiler_params=_params(3),
        name=name,
    )(x, g.reshape(1, d), shift, scale, w)


def _dwconv_kernel(*refs, taps, glu, post, tt, rb):
    n_in = 2 if glu else 1
    main = refs[:n_in]
    prev = refs[n_in:2 * n_in]
    nxt = refs[2 * n_in:3 * n_in]
    rest = refs[3 * n_in:]
    shifted = taps > SUBLANES
    if shifted:
        rest, sh_ref = rest[:-1], rest[-1]
    if post:
        w_ref, b_ref, lg_ref, lb_ref, o_ref, ext_ref = rest
    else:
        w_ref, b_ref, o_ref, ext_ref = rest
    i = pl.program_id(1)
    last = pl.num_programs(1) - 1

    def value(blocks):
        a = blocks[0][0].astype(F32)
        if glu:
            a = a * jax.nn.sigmoid(blocks[1][0].astype(F32))
        return a

    ext_ref[0:HALO, :] = jnp.where(i == 0, 0.0, value(prev))
    ext_ref[HALO:HALO + tt, :] = value(main)
    ext_ref[HALO + tt:2 * HALO + tt, :] = jnp.where(i == last, 0.0, value(nxt))

    pad = (taps - 1) // 2
    if shifted:
        span = sh_ref.shape[1]
        for s in range(1, SUBLANES):
            sh_ref[s - 1] = ext_ref[s:s + span, :]

    def window(start):
        s = start % SUBLANES
        if not shifted or s == 0:
            return ext_ref[start:start + rb, :]
        return sh_ref[s - 1, start - s:start - s + rb, :]

    for r in range(tt // rb):
        acc = jnp.zeros((rb, ext_ref.shape[1]), F32) + b_ref[...]
        for j in range(taps):
            acc = acc + w_ref[j:j + 1, :] * window(HALO - pad + j + r * rb)
        if post:
            mu = jnp.mean(acc, axis=-1, keepdims=True)
            cen = acc - mu
            var = jnp.mean(cen * cen, axis=-1, keepdims=True)
            y = cen * lax.rsqrt(var + EPS) * lg_ref[...] + lb_ref[...]
            acc = y * jax.nn.sigmoid(y)
        o_ref[0, r * rb:(r + 1) * rb, :] = acc.astype(o_ref.dtype)


def depthwise_conv(y, col, width, w, bias, glu=False, ln=None, out_dtype=F32, name="dwconv"):
    b, t, _ = y.shape
    taps = w.shape[0]
    tt = min(512 if ln is not None else 1024, t)
    cw = width if ln is not None else min(512, width)
    ncb = width // cw
    cb0 = col // cw
    hb = tt // HALO
    n_hb = t // HALO
    cols = [cb0, cb0 + ncb] if glu else [cb0]

    def main_spec(c0):
        return pl.BlockSpec((1, tt, cw), lambda bi, i, c: (bi, i, c0 + c))

    def prev_spec(c0):
        return pl.BlockSpec((1, HALO, cw), lambda bi, i, c: (bi, jnp.maximum(i * hb - 1, 0), c0 + c))

    def next_spec(c0):
        return pl.BlockSpec((1, HALO, cw),
                            lambda bi, i, c: (bi, jnp.minimum((i + 1) * hb, n_hb - 1), c0 + c))

    vec = lambda rows: pl.BlockSpec((rows, cw), lambda bi, i, c: (0, c))
    in_specs = ([main_spec(c0) for c0 in cols] + [prev_spec(c0) for c0 in cols]
                + [next_spec(c0) for c0 in cols] + [vec(taps), vec(1)])
    args = [y] * (3 * len(cols)) + [w, bias.reshape(1, width)]
    if ln is not None:
        in_specs += [vec(1), vec(1)]
        args += [ln[0].reshape(1, width), ln[1].reshape(1, width)]
    kern = functools.partial(_dwconv_kernel, taps=taps, glu=glu, post=ln is not None, tt=tt,
                             rb=min(128, tt))
    return pl.pallas_call(
        kern,
        out_shape=jax.ShapeDtypeStruct((b, t, width), out_dtype),
        grid=(b, t // tt, ncb),
        in_specs=in_specs,
        out_specs=pl.BlockSpec((1, tt, cw), lambda bi, i, c: (bi, i, c)),
        scratch_shapes=[pltpu.VMEM((tt + 2 * HALO, cw), F32)]
        + ([pltpu.VMEM((SUBLANES - 1, tt + 2 * HALO - SUBLANES, cw), F32)] if taps > SUBLANES else []),
        compiler_params=_params(3),
        name=name,
    )(*args)


def _swap_halves(x):
    lanes = lax.broadcasted_iota(jnp.int32, x.shape, 1)
    first = (lanes % HEAD_DIM) < (HEAD_DIM // 2)
    return jnp.where(first, pltpu.roll(x, LANE - HEAD_DIM // 2, 1), pltpu.roll(x, HEAD_DIM // 2, 1))


def _attn_kernel(*refs, lam_init, n_sub, t_c, has_lat):
    if has_lat:
        (q_ref, kc_ref, vc_ref, kx_ref, vx_ref, cq_ref, sq_ref, ck_ref, sk_ref, dl_ref, g_ref,
         o_ref, s_ref, p_ref, k_s, vt_s) = refs
    else:
        q_ref, kc_ref, vc_ref, dl_ref, g_ref, o_ref, s_ref, p_ref, k_s, vt_s = refs
    t_k = k_s.shape[0]
    tq = q_ref.shape[1]
    ts = tq // n_sub
    ck = 256

    @pl.when(pl.program_id(2) == 0)
    def _():
        k_s[0:t_c, :] = kc_ref[0]
        if has_lat:
            kx = kx_ref[0].astype(F32)
            k_s[t_c:, :] = (kx * ck_ref[...] + _swap_halves(kx) * sk_ref[...]).astype(BF16)
        for c in range(t_k // ck):
            lo = c * ck
            src = vc_ref[0, lo:lo + ck, :] if lo < t_c else vx_ref[0, lo - t_c:lo - t_c + ck, :]
            vt_s[:, lo:lo + ck] = src.astype(F32).T.astype(BF16)

    q = q_ref[0].astype(F32)
    if has_lat:
        q = q * cq_ref[...] + _swap_halves(q) * sq_ref[...]
    qt = (q * (HEAD_DIM ** -0.5 * LOG2E)).T
    rows = lax.broadcasted_iota(jnp.int32, qt.shape, 0)
    qts = (jnp.where(rows < HEAD_DIM, qt, 0.0).astype(BF16),
           jnp.where(rows >= HEAD_DIM, qt, 0.0).astype(BF16))

    kk = k_s[...]
    vt = vt_s[...]
    streams = [(mi, si) for si in range(n_sub) for mi in range(2)]
    ms = {}
    for mi, si in streams:
        s = jnp.dot(kk, qts[mi][:, si * ts:(si + 1) * ts], preferred_element_type=F32)
        s_ref[mi, si] = s
        ms[mi, si] = jnp.max(s, axis=0, keepdims=True)
    outs = {}
    for mi, si in streams:
        p = jnp.exp2(s_ref[mi, si] - ms[mi, si])
        l = jnp.sum(p, axis=0, keepdims=True)
        p_ref[mi, si] = p.astype(BF16)
        acc = jnp.dot(vt, p_ref[mi, si], preferred_element_type=F32)
        outs[mi, si] = acc / l

    dl = dl_ref[...]
    s1 = jnp.sum(dl[0:1] * dl[1:2], axis=-1, keepdims=True)
    s2 = jnp.sum(dl[2:3] * dl[3:4], axis=-1, keepdims=True)
    lam = jnp.exp(s1) - jnp.exp(s2) + lam_init
    for si in range(n_sub):
        ot = outs[0, si] - lam * outs[1, si]
        ot = ot * lax.rsqrt(jnp.mean(ot * ot, axis=0, keepdims=True) + EPS)
        o = ot.T
        o_ref[0, si * ts:(si + 1) * ts, :] = (o * g_ref[0] * (1.0 - lam_init)).astype(o_ref.dtype)


def diff_attention(y_q, y_c, y_x, rope, diff_lambda, subln_g, lam_init):
    b, t_q, _ = y_q.shape
    t_c = y_c.shape[1]
    has_lat = y_x is not None
    t_k = t_c + (y_x.shape[1] if has_lat else 0)
    tq = min(512, t_q)
    ts = min(256, tq)
    n_sub = tq // ts
    qb, kb, vb = COL_Q // HEAD_W, COL_K // HEAD_W, COL_V // HEAD_W
    head = lambda base, rows: pl.BlockSpec((1, rows, HEAD_W), lambda bi, hi, i: (bi, 0, base + hi))
    in_specs = [pl.BlockSpec((1, tq, HEAD_W), lambda bi, hi, i: (bi, i, qb + hi)),
                head(kb, t_c), head(vb, t_c)]
    args = [y_q, y_c, y_c]
    if has_lat:
        t_x = y_x.shape[1]
        in_specs += [head(kb, t_x), head(vb, t_x),
                     pl.BlockSpec((tq, HEAD_W), lambda bi, hi, i: (i, 0)),
                     pl.BlockSpec((tq, HEAD_W), lambda bi, hi, i: (i, 0)),
                     pl.BlockSpec((t_x, HEAD_W), lambda bi, hi, i: (0, 0)),
                     pl.BlockSpec((t_x, HEAD_W), lambda bi, hi, i: (0, 0))]
        args += [y_x, y_x, rope[0], rope[1], rope[0], rope[1]]
    in_specs += [pl.BlockSpec((4, HEAD_DIM), lambda bi, hi, i: (0, 0)),
                 pl.BlockSpec((1, 1, HEAD_W), lambda bi, hi, i: (hi, 0, 0))]
    args += [diff_lambda.astype(F32), subln_g.astype(F32).reshape(N_HEADS, 1, HEAD_W)]
    kern = functools.partial(_attn_kernel, lam_init=lam_init, n_sub=n_sub, t_c=t_c, has_lat=has_lat)
    return pl.pallas_call(
        kern,
        out_shape=jax.ShapeDtypeStruct((b, t_q, W_DIFF), BF16),
        grid=(b, N_HEADS, t_q // tq),
        in_specs=in_specs,
        out_specs=pl.BlockSpec((1, tq, HEAD_W), lambda bi, hi, i: (bi, i, hi)),
        scratch_shapes=[pltpu.VMEM((2, n_sub, t_k, ts), F32),
                        pltpu.VMEM((2, n_sub, t_k, ts), BF16),
                        pltpu.VMEM((t_k, HEAD_W), BF16),
                        pltpu.VMEM((HEAD_W, t_k), BF16)],
        compiler_params=_params(3),
        name="diff_attn",
    )(*args)


def rope_tables(t):
    rows = t // GRID_W
    row = np.repeat(np.arange(rows), GRID_W).astype(np.float64)
    col = np.tile(np.arange(GRID_W), rows).astype(np.float64)
    n_freq = HEAD_DIM // 4
    inv = (ROPE_BASE ** (-np.arange(n_freq, dtype=np.float32) / n_freq)).astype(np.float64)
    ang = np.concatenate([row[:, None] * inv, col[:, None] * inv], axis=-1)
    c, s = np.cos(ang), np.sin(ang)
    cos_t = np.tile(np.concatenate([c, c], axis=-1), (1, 2))
    sin_t = np.tile(np.concatenate([-s, s], axis=-1), (1, 2))
    return jnp.asarray(cos_t, F32), jnp.asarray(sin_t, F32)


@functools.lru_cache(maxsize=None)
def _dft_factors(length):
    n = 2 * length
    fb = min(FREQ_BLOCK, length)
    nb = length // fb
    kk = np.arange(length, dtype=np.int64).reshape(nb, 1, fb)
    k_rows = np.concatenate([kk, kk], axis=1)
    is_sin = np.zeros((nb, 2, fb), bool)
    is_sin[:, 1, :] = True
    k_rows[0, 1, 0] = length
    is_sin[0, 1, 0] = False
    k_rows = k_rows.reshape(-1)
    is_sin = is_sin.reshape(-1)
    n_hi = max(length // LANE, 1)
    n_lo = min(LANE, length)
    a_lo = 2.0 * np.pi * ((k_rows[:, None] * np.arange(n_lo)[None, :]) % n) / n
    a_hi = 2.0 * np.pi * ((k_rows[:, None] * (LANE * np.arange(n_hi))[None, :]) % n) / n
    c_lo, s_lo, c_hi, s_hi = np.cos(a_lo), np.sin(a_lo), np.cos(a_hi), np.sin(a_hi)
    p = np.where(is_sin[:, None], s_hi, c_hi)
    q = np.where(is_sin[:, None], c_hi, -s_hi)
    scale = np.full((2 * length,), 2.0 / n)
    scale[0] = 1.0 / n
    scale[fb] = 1.0 / n
    f32 = lambda a: np.asarray(a, np.float32)
    return f32(p), f32(q), f32(c_lo), f32(s_lo), f32(scale.reshape(-1, 1))


def dft_tables(length):
    p, q, c_lo, s_lo, _ = _dft_factors(length)
    fwd = (jnp.asarray(p)[:, :, None] * jnp.asarray(c_lo)[:, None, :]
           + jnp.asarray(q)[:, :, None] * jnp.asarray(s_lo)[:, None, :])
    fwd = fwd.reshape(2 * length, length).astype(BF16)
    return fwd, fwd.T


def _cmul(za, zb, ha, hb, first_rows):
    ya = za * ha - zb * hb
    yb = za * hb + zb * ha
    ya = jnp.where(first_rows, za * ha, ya)
    yb = jnp.where(first_rows, zb * hb, yb)
    return ya, yb


def _fwd_kernel(f_ref, z_ref, h_ref, c_ref, y_ref, zb_ref, *, fb):
    @pl.when(pl.program_id(1) == 0)
    def _():
        zb_ref[...] = z_ref[0].astype(BF16)

    spec = jnp.dot(f_ref[...], zb_ref[...], preferred_element_type=F32)
    za, zb = spec[:fb], spec[fb:]
    h = h_ref[...]
    rows = lax.broadcasted_iota(jnp.int32, za.shape, 0)
    first = jnp.logical_and(rows == 0, pl.program_id(1) == 0)
    ya, yb = _cmul(za, zb, h[:fb], h[fb:], first)
    c = c_ref[...]
    y_ref[0, :fb, :] = (ya * c[:fb]).astype(y_ref.dtype)
    y_ref[0, fb:, :] = (yb * c[fb:]).astype(y_ref.dtype)


def hyena_forward(z, z_col, hspec, order, fwd, length):
    b = z.shape[0]
    w = W_HYENA
    scale = jnp.asarray(_dft_factors(length)[4])
    fb = min(FREQ_BLOCK, length)
    kern = functools.partial(_fwd_kernel, fb=fb)
    return pl.pallas_call(
        kern,
        out_shape=jax.ShapeDtypeStruct((b, 2 * length, w), BF16),
        grid=(b, length // fb),
        in_specs=[pl.BlockSpec((2 * fb, length), lambda bi, i: (i, 0)),
                  pl.BlockSpec((1, length, w), lambda bi, i: (bi, 0, z_col)),
                  pl.BlockSpec((2 * fb, w), lambda bi, i: (i, order)),
                  pl.BlockSpec((2 * fb, 1), lambda bi, i: (i, 0))],
        out_specs=pl.BlockSpec((1, 2 * fb, w), lambda bi, i: (bi, i, 0)),
        scratch_shapes=[pltpu.VMEM((length, w), BF16)],
        compiler_params=_params(2),
        name="hyena_fwd",
    )(fwd, z, hspec, scale)


def _inv_kernel(g_ref, y_ref, z_ref, gate_ref, skip_ref, o_ref):
    conv = jnp.dot(g_ref[...], y_ref[0], preferred_element_type=F32)
    o_ref[0] = (gate_ref[0] * (conv + z_ref[0] * skip_ref[...])).astype(o_ref.dtype)


def hyena_inverse(y, z, z_col, gate, gate_col, skip, inv, length, out_dtype):
    b = z.shape[0]
    w = W_HYENA
    tm = min(512, length)
    return pl.pallas_call(
        _inv_kernel,
        out_shape=jax.ShapeDtypeStruct((b, length, w), out_dtype),
        grid=(b, length // tm),
        in_specs=[pl.BlockSpec((tm, 2 * length), lambda bi, i: (i, 0)),
                  pl.BlockSpec((1, 2 * length, w), lambda bi, i: (bi, 0, 0)),
                  pl.BlockSpec((1, tm, w), lambda bi, i: (bi, i, z_col)),
                  pl.BlockSpec((1, tm, w), lambda bi, i: (bi, i, gate_col)),
                  pl.BlockSpec((1, w), lambda bi, i: (0, 0))],
        out_specs=pl.BlockSpec((1, tm, w), lambda bi, i: (bi, i, 0)),
        compiler_params=_params(2),
        name="hyena_inv",
    )(inv, y, z, gate, skip.reshape(1, w))


def _filt_kernel(feat_ref, w1_ref, b1_ref, w2_ref, b2_ref, w3f_ref, w3b_ref, dl_ref,
                 taps_ref, hf_ref, hb_ref, *, rb):
    hi = lax.Precision.HIGHEST
    length, ct = hf_ref.shape
    n_blocks = length // rb

    def taps(i, norm):
        r0 = pl.multiple_of(i * rb, rb)
        feats = feat_ref[pl.ds(r0, rb), :]
        h = jnp.sin(jnp.dot(feats, w1_ref[...], precision=hi, preferred_element_type=F32)
                    + b1_ref[...])
        h = jnp.sin(jnp.dot(h, w2_ref[...], precision=hi, preferred_element_type=F32)
                    + b2_ref[...])
        decay = jnp.exp(-feats[:, 0:1] * dl_ref[...])
        hf = jnp.dot(h, w3f_ref[...], precision=hi, preferred_element_type=F32) * decay
        hb = jnp.dot(h, w3b_ref[...], precision=hi, preferred_element_type=F32) * decay
        rows = lax.broadcasted_iota(jnp.int32, hb.shape, 0) + r0
        hb = jnp.where(rows == 0, 0.0, hb)
        hf_ref[pl.ds(r0, rb), :] = hf
        hb_ref[pl.ds(r0, rb), :] = hb
        return (norm + jnp.sum(jnp.abs(hf), axis=0, keepdims=True)
                + jnp.sum(jnp.abs(hb), axis=0, keepdims=True))

    norm = lax.fori_loop(0, n_blocks, taps, jnp.zeros((1, ct), F32))
    inv = 1.0 / norm

    def normalise(i, carry):
        r0 = pl.multiple_of(i * rb, rb)
        taps_ref[0, pl.ds(r0, rb), :] = hf_ref[pl.ds(r0, rb), :] * inv
        taps_ref[1, pl.ds(r0, rb), :] = hb_ref[pl.ds(r0, rb), :] * inv
        return carry

    lax.fori_loop(0, n_blocks, normalise, 0)


def hyena_filter_taps(length, filt_w1, filt_b1, filt_w2, filt_b2, filt_w3):
    w = W_HYENA
    ct = 256
    pos = np.arange(length, dtype=np.float64)
    t = pos / max(length - 1, 1)
    bands = np.linspace(1e-4, HYENA_BANDS - 1, HYENA_BANDS)
    ang = (2.0 * math.pi * pos / length)[:, None] * bands[None, :]
    feats = np.concatenate([t[:, None], np.cos(ang), -np.sin(ang)], axis=-1).astype(np.float32)
    deltas = np.abs(np.linspace(math.log(HYENA_TARGET) / HYENA_SLOW_PCT,
                                math.log(HYENA_TARGET) / HYENA_FAST_PCT, w)).astype(np.float32)
    emb, nf = filt_w1.shape
    npc = w // ct
    full = lambda shape: pl.BlockSpec(shape, lambda o, c: (0,) * len(shape))
    kern = functools.partial(_filt_kernel, rb=min(256, length))
    return pl.pallas_call(
        kern,
        out_shape=jax.ShapeDtypeStruct((HYENA_ORDER, 2, length, w), F32),
        grid=(HYENA_ORDER, npc),
        in_specs=[full((length, emb)), full((emb, nf)), full((1, nf)), full((nf, nf)), full((1, nf)),
                  pl.BlockSpec((nf, ct), lambda o, c: (0, o * 2 * npc + c)),
                  pl.BlockSpec((nf, ct), lambda o, c: (0, o * 2 * npc + npc + c)),
                  pl.BlockSpec((1, ct), lambda o, c: (0, c))],
        out_specs=pl.BlockSpec((None, 2, length, ct), lambda o, c: (o, 0, 0, c)),
        scratch_shapes=[pltpu.VMEM((length, ct), F32),
                        pltpu.VMEM((length, ct), F32)],
        compiler_params=_params(2),
        name="hyena_filt",
    )(jnp.asarray(feats), filt_w1, filt_b1.reshape(1, nf), filt_w2, filt_b2.reshape(1, nf),
      filt_w3, filt_w3, jnp.asarray(deltas).reshape(1, w))


def _hspec_kernel(f_ref, t_ref, o_ref, *, fb):
    f = f_ref[...]
    hs = (t_ref[0] + t_ref[1]).astype(BF16)
    hd = (t_ref[0] - t_ref[1]).astype(BF16)
    a = jnp.dot(f, hs, preferred_element_type=F32)
    bd = jnp.dot(f[fb:], hd, preferred_element_type=F32)
    rows = lax.broadcasted_iota(jnp.int32, bd.shape, 0)
    first = jnp.logical_and(rows == 0, pl.program_id(1) == 0)
    o_ref[:fb, :] = a[:fb]
    o_ref[fb:, :] = jnp.where(first, a[fb:], bd)


def hyena_filter_spectrum(taps, fwd, length):
    fb = min(FREQ_BLOCK, length)
    w = W_HYENA
    kern = functools.partial(_hspec_kernel, fb=fb)
    return pl.pallas_call(
        kern,
        out_shape=jax.ShapeDtypeStruct((2 * length, HYENA_ORDER * w), F32),
        grid=(HYENA_ORDER, length // fb),
        in_specs=[pl.BlockSpec((2 * fb, length), lambda j, i: (i, 0)),
                  pl.BlockSpec((None, 2, length, w), lambda j, i: (j, 0, 0, 0))],
        out_specs=pl.BlockSpec((2 * fb, w), lambda j, i: (i, j)),
        compiler_params=_params(2),
        name="hyena_hspec",
    )(fwd, taps)


FFT_N1 = 64
FFT_N2 = 128
FFT_N = FFT_N1 * FFT_N2
FFT_KB = 8
FFT_TN = 8192


@functools.lru_cache(maxsize=None)
def _fft_tables():
    n1h = FFT_N1 // 2
    k1 = np.arange(FFT_N1)
    ang = 2.0 * np.pi * np.outer(k1, np.arange(n1h)) / FFT_N1
    c, s = np.cos(ang), np.sin(ang)
    f1 = np.block([[c, s], [-s, c]])
    f1i = np.block([[c.T, -s.T], [s.T, c.T]]) / FFT_N
    k2 = np.arange(FFT_N2)
    n2 = np.arange(FFT_N2)
    kfull = k1[:, None, None] + FFT_N1 * k2[None, :, None]
    ang2 = 2.0 * np.pi * ((kfull * n2[None, None, :]) % FFT_N) / FFT_N
    gc, gs = np.cos(ang2), np.sin(ang2)
    g = np.concatenate([np.concatenate([gc, gs], axis=2),
                        np.concatenate([-gs, gc], axis=2)], axis=1)
    f32 = lambda a: np.asarray(a, np.float32)
    return f32(f1), f32(g), f32(np.transpose(g, (0, 2, 1))), f32(f1i)


def _fft_s1_kernel(f_ref, x_ref, o_ref):
    o_ref[0] = jnp.dot(f_ref[...], x_ref[0].astype(BF16),
                       preferred_element_type=F32).astype(o_ref.dtype)


def fft_stage1(x):
    p, rows, cols = x.shape
    f1 = jnp.asarray(_fft_tables()[0][:, :rows], BF16)
    return pl.pallas_call(
        _fft_s1_kernel,
        out_shape=jax.ShapeDtypeStruct((p, 2 * FFT_N1, cols), BF16),
        grid=(p, cols // FFT_TN),
        in_specs=[pl.BlockSpec((2 * FFT_N1, rows), lambda pi, j: (0, 0)),
                  pl.BlockSpec((1, rows, FFT_TN), lambda pi, j: (pi, 0, j))],
        out_specs=pl.BlockSpec((1, 2 * FFT_N1, FFT_TN), lambda pi, j: (pi, 0, j)),
        compiler_params=_params(2),
        name="fft_s1",
    )(f1, x)


def _fft_s2(g_ref, a_ref, j):
    rhs = jnp.concatenate([a_ref[0, 0, j], a_ref[0, 1, j]], axis=0)
    return jnp.dot(g_ref[j], rhs, preferred_element_type=F32)


def _fft_s2_filter_kernel(g_ref, ap_ref, aq_ref, h_ref):
    half = FFT_N2
    for j in range(FFT_KB):
        sp = _fft_s2(g_ref, ap_ref, j)
        sq = _fft_s2(g_ref, aq_ref, j)
        h_ref[0, j, :half, :] = sp[:half] + sq[:half]
        h_ref[0, j, half:, :] = sp[half:] - sq[half:]


def _fft_s2_data_kernel(g_ref, a_ref, h_ref, y_ref):
    half = FFT_N2
    for j in range(FFT_KB):
        s = _fft_s2(g_ref, a_ref, j)
        sre, sim = s[:half], s[half:]
        hre, him = h_ref[0, j, :half, :], h_ref[0, j, half:, :]
        y_ref[0, j, :half, :] = (sre * hre - sim * him).astype(y_ref.dtype)
        y_ref[0, j, half:, :] = (sre * him + sim * hre).astype(y_ref.dtype)


def fft_filter_spectrum(a):
    w = W_HYENA
    g = jnp.asarray(_fft_tables()[1], BF16)
    a5 = a.reshape(2 * HYENA_ORDER, 2, FFT_N1, FFT_N2, w)
    taps_spec = lambda d: pl.BlockSpec((1, 2, FFT_KB, FFT_N2, w), lambda j, o: (2 * o + d, 0, j, 0, 0))
    return pl.pallas_call(
        _fft_s2_filter_kernel,
        out_shape=jax.ShapeDtypeStruct((HYENA_ORDER, FFT_N1, 2 * FFT_N2, w), F32),
        grid=(FFT_N1 // FFT_KB, HYENA_ORDER),
        in_specs=[pl.BlockSpec((FFT_KB, 2 * FFT_N2, 2 * FFT_N2), lambda j, o: (j, 0, 0)),
                  taps_spec(0), taps_spec(1)],
        out_specs=pl.BlockSpec((1, FFT_KB, 2 * FFT_N2, w), lambda j, o: (o, j, 0, 0)),
        compiler_params=_params(2),
        name="fft_s2_filter",
    )(g, a5, a5)


def fft_stage2(a, hspec, order):
    p = a.shape[0]
    w = W_HYENA
    g = jnp.asarray(_fft_tables()[1], BF16)
    a5 = a.reshape(p, 2, FFT_N1, FFT_N2, w)
    return pl.pallas_call(
        _fft_s2_data_kernel,
        out_shape=jax.ShapeDtypeStruct((p, FFT_N1, 2 * FFT_N2, w), BF16),
        grid=(FFT_N1 // FFT_KB, p),
        in_specs=[pl.BlockSpec((FFT_KB, 2 * FFT_N2, 2 * FFT_N2), lambda j, pi: (j, 0, 0)),
                  pl.BlockSpec((1, 2, FFT_KB, FFT_N2, w), lambda j, pi: (pi, 0, j, 0, 0)),
                  pl.BlockSpec((1, FFT_KB, 2 * FFT_N2, w), lambda j, pi: (order, j, 0, 0))],
        out_specs=pl.BlockSpec((1, FFT_KB, 2 * FFT_N2, w), lambda j, pi: (pi, j, 0, 0)),
        compiler_params=_params(2),
        name="fft_s2",
    )(g, a5, hspec)


def _ifft_a_kernel(gt_ref, y_ref, b_ref):
    half = FFT_N2
    for j in range(FFT_KB):
        b = jnp.dot(gt_ref[j], y_ref[0, j], preferred_element_type=F32)
        b_ref[0, 0, j] = b[:half].astype(b_ref.dtype)
        b_ref[0, 1, j] = b[half:].astype(b_ref.dtype)


def ifft_stage_a(y):
    p = y.shape[0]
    w = W_HYENA
    gt = jnp.asarray(_fft_tables()[2], BF16)
    out = pl.pallas_call(
        _ifft_a_kernel,
        out_shape=jax.ShapeDtypeStruct((p, 2, FFT_N1, FFT_N2, w), BF16),
        grid=(FFT_N1 // FFT_KB, p),
        in_specs=[pl.BlockSpec((FFT_KB, 2 * FFT_N2, 2 * FFT_N2), lambda j, pi: (j, 0, 0)),
                  pl.BlockSpec((1, FFT_KB, 2 * FFT_N2, w), lambda j, pi: (pi, j, 0, 0))],
        out_specs=pl.BlockSpec((1, 2, FFT_KB, FFT_N2, w), lambda j, pi: (pi, 0, j, 0, 0)),
        compiler_params=_params(2),
        name="ifft_a",
    )(gt, y)
    return out.reshape(p, 2 * FFT_N1, FFT_N2 * w)


def _ifft_b_kernel(f_ref, b_ref, z_ref, gate_ref, skip_ref, o_ref):
    conv = jnp.dot(f_ref[...], b_ref[0], preferred_element_type=F32)
    o_ref[0] = (gate_ref[0] * (conv + z_ref[0] * skip_ref[...])).astype(o_ref.dtype)


def ifft_stage_b(b, z, gate, skip, out_dtype):
    p, _, cols = b.shape
    f1i = jnp.asarray(_fft_tables()[3], BF16)
    skip_flat = jnp.tile(skip.reshape(1, W_HYENA), (1, FFT_N2))
    return pl.pallas_call(
        _ifft_b_kernel,
        out_shape=jax.ShapeDtypeStruct((p, FFT_N1, cols), out_dtype),
        grid=(p, cols // FFT_TN),
        in_specs=[pl.BlockSpec((FFT_N1, 2 * FFT_N1), lambda pi, j: (0, 0)),
                  pl.BlockSpec((1, 2 * FFT_N1, FFT_TN), lambda pi, j: (pi, 0, j)),
                  pl.BlockSpec((1, FFT_N1, FFT_TN), lambda pi, j: (pi, 0, j)),
                  pl.BlockSpec((1, FFT_N1, FFT_TN), lambda pi, j: (pi, 0, j)),
                  pl.BlockSpec((1, FFT_TN), lambda pi, j: (0, j))],
        out_specs=pl.BlockSpec((1, FFT_N1, FFT_TN), lambda pi, j: (pi, 0, j)),
        compiler_params=_params(2),
        name="ifft_b",
    )(f1i, b, z, gate, skip_flat)


def hyena_long_conv(v, x1, x2, hyena_skip, filt):
    b, length, w = v.shape
    taps = hyena_filter_taps(length, *filt)
    if 2 * length == FFT_N and b % 2 == 0:
        pairs = b // 2
        flat = lambda t: t.reshape(pairs, FFT_N1, FFT_N2 * w)
        hspec = fft_filter_spectrum(
            fft_stage1(taps.reshape(2 * HYENA_ORDER, FFT_N1 // 2, FFT_N2 * w)))
        z = flat(v)
        for o, gate in enumerate((x1, x2)):
            y = fft_stage2(fft_stage1(z), hspec, o)
            z = ifft_stage_b(ifft_stage_a(y), z, flat(gate), hyena_skip[o],
                             F32 if o == 0 else BF16)
        return z.reshape(b, length, w)
    fwd, inv = dft_tables(length)
    hspec = hyena_filter_spectrum(taps, fwd, length)
    y = hyena_forward(v, 0, hspec, 0, fwd, length)
    z = hyena_inverse(y, v, 0, x1, 0, hyena_skip[0], inv, length, F32)
    y = hyena_forward(z, 0, hspec, 1, fwd, length)
    return hyena_inverse(y, z, 0, x2, 0, hyena_skip[1], inv, length, BF16)


def _merge_kernel(ya_ref, zb_ref, oc_ref, ga_ref, gb_ref, gc_ref, bg_ref, wa_ref, wb_ref, wc_ref,
                  wo_ref, x_ref, mg_ref, o_ref):
    d = D_MODEL

    def branch(act_ref, w_ref, g_ref, k):
        br = jnp.dot(act_ref[0], w_ref[...], preferred_element_type=F32)
        return jax.nn.sigmoid(g_ref[0].astype(F32) + bg_ref[:, k * d:(k + 1) * d]) * br

    m = (branch(ya_ref, wa_ref, ga_ref, 0) + branch(zb_ref, wb_ref, gb_ref, 1)
         + branch(oc_ref, wc_ref, gc_ref, 2))
    out = jnp.dot(m.astype(BF16), wo_ref[...], preferred_element_type=F32)
    o_ref[0] = x_ref[0] + mg_ref[0] * out


def gated_merge(ya, zb, oc, y, b_gate, w_a_out, w_b_out, w_c_out, w_o, x, mod_gate):
    b, t, d = x.shape
    tm = min(256, t)
    per_batch = mod_gate.shape[0] > 1
    row = lambda width, cb: pl.BlockSpec((1, tm, width), lambda bi, i: (bi, i, cb))
    const = lambda shape: pl.BlockSpec(shape, lambda bi, i: (0,) * len(shape),
                                       pipeline_mode=pl.Buffered(1))
    g0 = COL_GATE // d
    return pl.pallas_call(
        _merge_kernel,
        out_shape=jax.ShapeDtypeStruct((b, t, d), F32),
        grid=(b, t // tm),
        in_specs=[row(W_CONV, 0), row(W_HYENA, 0), row(W_DIFF, 0),
                  row(d, g0), row(d, g0 + 1), row(d, g0 + 2),
                  const((1, 3 * d)), const((W_CONV, d)), const((W_HYENA, d)), const((W_DIFF, d)),
                  const((d, d)), row(d, 0),
                  pl.BlockSpec((1, 1, d), (lambda bi, i: (bi, 0, 0)) if per_batch
                               else (lambda bi, i: (0, 0, 0)))],
        out_specs=row(d, 0),
        compiler_params=_params(2),
        name="gated_merge",
    )(ya, zb, oc, y, y, y, b_gate.reshape(1, 3 * d), w_a_out.astype(BF16), w_b_out.astype(BF16),
      w_c_out.astype(BF16), w_o.astype(BF16), x, mod_gate)


def _ffn_down_kernel(*refs, tm, tk, final):
    (a_ref, v_ref, ap_ref, vp_ref, an_ref, vn_ref, cw_ref, cb_ref, w_ref, x_ref, mg_ref) = refs[:11]
    o_ref, ext_ref, hid_ref = refs[-3:]
    i = pl.program_id(1)
    last_i = pl.num_programs(1) - 1

    def conv(slot, c_ref, p_ref, n_ref, c0, w0):
        ext_ref[slot, 0:HALO, :] = jnp.where(i == 0, 0.0, p_ref[0, :, c0:c0 + tk].astype(F32))
        ext_ref[slot, HALO:HALO + tm, :] = c_ref[0, :, c0:c0 + tk].astype(F32)
        ext_ref[slot, HALO + tm:2 * HALO + tm, :] = jnp.where(
            i == last_i, 0.0, n_ref[0, :, c0:c0 + tk].astype(F32))
        w = cw_ref[:, w0:w0 + tk]
        return (w[0:1] * ext_ref[slot, HALO - 1:HALO - 1 + tm, :]
                + w[1:2] * ext_ref[slot, HALO:HALO + tm, :]
                + w[2:3] * ext_ref[slot, HALO + 1:HALO + 1 + tm, :] + cb_ref[:, w0:w0 + tk])

    out = None
    for kc in range(D_FF // tk):
        c0 = kc * tk
        a = conv((2 * kc) % 4, a_ref, ap_ref, an_ref, c0, c0)
        v = conv((2 * kc + 1) % 4, v_ref, vp_ref, vn_ref, c0, D_FF + c0)
        hid_ref[kc % 2] = (a * jax.nn.sigmoid(a) * v).astype(BF16)
        part = jnp.dot(hid_ref[kc % 2], w_ref[c0:c0 + tk, :], preferred_element_type=F32)
        out = part if out is None else out + part
    out = x_ref[0] + mg_ref[0] * out
    if final:
        out = _rms(out, refs[11][...])
    o_ref[0] = out


def ffn_down(u, conv_w, conv_b, w_down, layer, x, mod_gate, final_g=None):
    b, t, d = x.shape
    tk = 256
    tm = min(256, t)
    hb = tm // HALO
    n_hb = t // HALO
    per_batch = mod_gate.shape[0] > 1
    main = lambda cb: pl.BlockSpec((1, tm, D_FF), lambda bi, i: (bi, i, cb))
    prev = lambda cb: pl.BlockSpec((1, HALO, D_FF), lambda bi, i: (bi, jnp.maximum(i * hb - 1, 0), cb))
    nxt = lambda cb: pl.BlockSpec((1, HALO, D_FF),
                                  lambda bi, i: (bi, jnp.minimum((i + 1) * hb, n_hb - 1), cb))
    const = lambda shape: pl.BlockSpec(shape, lambda bi, i: (0,) * len(shape))
    in_specs = [main(0), main(1), prev(0), prev(1), nxt(0), nxt(1),
                const((3, 2 * D_FF)), const((1, 2 * D_FF)),
                pl.BlockSpec((None, D_FF, d), lambda bi, i: (layer, 0, 0),
                             pipeline_mode=pl.Buffered(1)),
                pl.BlockSpec((1, tm, d), lambda bi, i: (bi, i, 0)),
                pl.BlockSpec((1, 1, d), (lambda bi, i: (bi, 0, 0)) if per_batch
                             else (lambda bi, i: (0, 0, 0)))]
    args = [u, u, u, u, u, u, conv_w, conv_b.reshape(1, 2 * D_FF), w_down, x, mod_gate]
    if final_g is not None:
        in_specs.append(const((1, d)))
        args.append(final_g.reshape(1, d))
    kern = functools.partial(_ffn_down_kernel, tm=tm, tk=tk, final=final_g is not None)
    return pl.pallas_call(
        kern,
        out_shape=jax.ShapeDtypeStruct((b, t, d), F32),
        grid=(b, t // tm),
        in_specs=in_specs,
        out_specs=pl.BlockSpec((1, tm, d), lambda bi, i: (bi, i, 0)),
        scratch_shapes=[pltpu.VMEM((4, tm + 2 * HALO, tk), F32),
                        pltpu.VMEM((2, tm, tk), BF16)],
        compiler_params=_params(2),
        name="ffn_down",
    )(*args)


def mixer_sublayer(x, y, y_c, y_lat, rope, p, mod_gate, lam_init):
    ya = depthwise_conv(y, COL_A, W_CONV, p['conv_a_w'], p['conv_a_b'], glu=True,
                        ln=(p['ln_a_g'], p['ln_a_b']), out_dtype=BF16, name="conformer_conv")
    v, x1, x2 = [depthwise_conv(y, COL_B + k * W_HYENA, W_HYENA,
                                p['short_b_w'][:, k * W_HYENA:(k + 1) * W_HYENA],
                                p['short_b_b'][k * W_HYENA:(k + 1) * W_HYENA], name="hyena_short_conv")
                 for k in range(3)]
    filt = (p['filt_w1'], p['filt_b1'], p['filt_w2'], p['filt_b2'], p['filt_w3'])
    zb = hyena_long_conv(v, x1, x2, p['hyena_skip'], filt)
    oc = diff_attention(y, y_c, y_lat, rope, p['diff_lambda'], p['subln_g'], lam_init)
    return gated_merge(ya, zb, oc, y, p['b_gate'], p['w_a_out'], p['w_b_out'], p['w_c_out'],
                       p['w_o'], x, mod_gate)


def kernel(x, c, ctx, c_ctx, w_ada, b_ada, norm1_g, norm2_g, w_in, b_gate, conv_a_w, conv_a_b, ln_a_g, ln_a_b, w_a_out, short_b_w, short_b_b, filt_w1, filt_b1, filt_w2, filt_b2, filt_w3, hyena_skip, w_b_out, diff_lambda, subln_g, w_c_out, w_o, w_up, conv_f_w, conv_f_b, w_down, final_g):
    batch = x.shape[0]
    rope = rope_tables(x.shape[1])
    cc = jnp.concatenate([c, c_ctx[None, :], jnp.zeros((8 - batch - 1, D_MODEL), F32)], axis=0)
    gate_rot = N_MIX // TN
    w_in_b, w_up_b, w_down_b = w_in.astype(BF16), w_up.astype(BF16), w_down.astype(BF16)
    for l in range(DEPTH):
        p = {
            'b_gate': b_gate[l],
            'conv_a_w': conv_a_w[l], 'conv_a_b': conv_a_b[l],
            'ln_a_g': ln_a_g[l], 'ln_a_b': ln_a_b[l], 'w_a_out': w_a_out[l],
            'short_b_w': short_b_w[l], 'short_b_b': short_b_b[l],
            'filt_w1': filt_w1[l], 'filt_b1': filt_b1[l], 'filt_w2': filt_w2[l],
            'filt_b2': filt_b2[l], 'filt_w3': filt_w3[l],
            'hyena_skip': hyena_skip[l], 'w_b_out': w_b_out[l],
            'diff_lambda': diff_lambda[l], 'subln_g': subln_g[l], 'w_c_out': w_c_out[l],
            'w_o': w_o[l],
        }
        update_ctx = l < DEPTH - 1
        last = l == DEPTH - 1
        lam_init = 0.8 - 0.6 * math.exp(-0.3 * l)
        mod = ada_projection(cc, w_ada, l, b_ada[l])
        mx = [mod[:batch, None, j * D_MODEL:(j + 1) * D_MODEL] for j in range(6)]
        mc = [mod[batch:batch + 1, None, j * D_MODEL:(j + 1) * D_MODEL] for j in range(6)]

        y_x = norm_mod_matmul(x, norm1_g[l], mx[0], mx[1], w_in_b, l, gate_rot, "mm_in_x")
        t_c = ctx.shape[1]
        merged = lambda a: a.reshape(1, batch * t_c, a.shape[2])
        split = lambda a: a.reshape(batch, t_c, a.shape[2])
        y_c = split(norm_mod_matmul(merged(ctx), norm1_g[l], mc[0], mc[1], w_in_b, l, gate_rot,
                                    "mm_in_c"))
        x = mixer_sublayer(x, y_x, y_c, y_x, rope, p, mx[2], lam_init)
        u = norm_mod_matmul(x, norm2_g[l], mx[3], mx[4], w_up_b, l, 0, "mm_up_x")
        x = ffn_down(u, conv_f_w[l], conv_f_b[l], w_down_b, l, x, mx[5], final_g if last else None)
        if update_ctx:
            ctx = mixer_sublayer(ctx, y_c, y_c, None, None, p, mc[2], lam_init)
            u = split(norm_mod_matmul(merged(ctx), norm2_g[l], mc[3], mc[4], w_up_b, l, 0, "mm_up_c"))
            ctx = ffn_down(u, conv_f_w[l], conv_f_b[l], w_down_b, l, ctx, mc[5])
    return x
```
